```python
import math
import jax, jax.numpy as jnp
from jax import lax
import numpy as np

D_MODEL = 1024
BATCH = 8
SEQ = 2048
DEPTH = 4
DEC_BATCH = 128
DEC_SEQ = 1
PAST_LEN = 2048
PAGE_SIZE = 128

N_MIXERS = 3
N_SCONV_LAYERS = (DEPTH + 2) // 3
N_SSD_LAYERS = (DEPTH + 1) // 3
N_ATTN_LAYERS = DEPTH // 3
RMS_EPS = 1e-6
SCONV_WIDTH = 3
SSD_EXPAND = 2
D_INNER = SSD_EXPAND * D_MODEL
SSD_HEAD_DIM = 64
SSD_HEADS = D_INNER // SSD_HEAD_DIM
SSD_GROUPS = 4
SSD_HEADS_PER_GROUP = SSD_HEADS // SSD_GROUPS
SSD_STATE = 128
SSD_CONV_WIDTH = 4
SSD_CONV_DIM = D_INNER + 2 * SSD_GROUPS * SSD_STATE
SSD_IN_DIM = D_INNER + SSD_CONV_DIM + SSD_HEADS
SSD_CHUNK = 128
ATTN_HEAD_DIM = 64
ATTN_HEADS = D_MODEL // ATTN_HEAD_DIM
MOBA_BLOCK = 256
MOBA_TOP_K = 3
MOBA_Q_BLOCK = 16
ROPE_THETA = 500000.0
ROT_DIM = ATTN_HEAD_DIM // 4
D_FF = 2816
FFN_CONV_WIDTH = 3

kernel_name = "hybrid_sconv_ssd_moba_convffn_step"


def rmsnorm(x, w):
    xf = x.astype(jnp.float32)
    inv = lax.rsqrt(jnp.mean(xf * xf, axis=-1, keepdims=True) + RMS_EPS)
    return (xf * inv).astype(x.dtype) * w


def causal_dwconv(u, w, past):
    width = w.shape[0]
    L = u.shape[1]
    full = jnp.concatenate([past.astype(u.dtype), u], axis=1)
    out = sum(full[:, k:k + L] * w[k] for k in range(width))
    return out, full[:, L:]


def short_conv_mixer(h, w_in, conv_w, w_out, past):
    gate_out, gate_in, val = jnp.split(h @ w_in, 3, axis=-1)
    c, new_past = causal_dwconv(gate_in * val, conv_w, past)
    return (gate_out * c) @ w_out, new_past


def ssd_scan(x, dt, a, b_in, c_in, init_state):
    bsz, L = x.shape[:2]
    q = min(SSD_CHUNK, L)
    pad = (-L) % q
    if pad:
        padf = lambda t: jnp.pad(t, [(0, 0), (0, pad)] + [(0, 0)] * (t.ndim - 2))
        x, dt, b_in, c_in = padf(x), padf(dt), padf(b_in), padf(c_in)
    nc = (L + pad) // q
    G, E, P, N = SSD_GROUPS, SSD_HEADS_PER_GROUP, SSD_HEAD_DIM, SSD_STATE
    xc = x.reshape(bsz, nc, q, G, E, P).astype(jnp.float32)
    dtc = dt.reshape(bsz, nc, q, G, E).astype(jnp.float32)
    bc = b_in.reshape(bsz, nc, q, G, N).astype(jnp.float32)
    cc = c_in.reshape(bsz, nc, q, G, N).astype(jnp.float32)
    a_cum = jnp.cumsum(dtc * a.reshape(G, E), axis=2)
    xdt = xc * dtc[..., None]
    seg = a_cum[:, :, :, None] - a_cum[:, :, None, :]
    causal = jnp.tril(jnp.ones((q, q), bool))[:, :, None, None]
    decay = jnp.exp(jnp.where(causal, seg, -jnp.inf))
    cb = jnp.einsum('bclgn,bcsgn->bclsg', cc, bc)
    y_diag = jnp.einsum('bclsg,bclsge,bcsgep->bclgep', cb, decay, xdt)
    decay_to_end = jnp.exp(a_cum[:, :, -1:] - a_cum)
    states = jnp.einsum('bcsgn,bcsge,bcsgep->bcgepn', bc, decay_to_end, xdt)
    chunk_decay = jnp.exp(a_cum[:, :, -1])

    def step(carry, inp):
        st, dec = inp
        return carry * dec[..., None, None] + st, carry

    init = init_state.astype(jnp.float32).reshape(bsz, G, E, P, N)
    final, prev = lax.scan(step, init, (jnp.moveaxis(states, 1, 0), jnp.moveaxis(chunk_decay, 1, 0)))
    prev = jnp.moveaxis(prev, 0, 1)
    y_off = jnp.einsum('bclgn,bcgepn,bclge->bclgep', cc, prev, jnp.exp(a_cum))
    y = (y_diag + y_off).reshape(bsz, nc * q, SSD_HEADS, P)[:, :L]
    return y, final.reshape(bsz, SSD_HEADS, P, N)


def ssd_mixer(h, w_in, conv_w, conv_b, dt_bias, a_log, d_skip, norm_w, w_out, ssm_state, conv_past):
    bsz, L, _ = h.shape
    z, xbc, dt = jnp.split(h @ w_in, [D_INNER, D_INNER + SSD_CONV_DIM], axis=-1)
    xbc_c, new_conv = causal_dwconv(xbc, conv_w, conv_past)
    xbc_c = jax.nn.silu(xbc_c + conv_b)
    xs, b_in, c_in = jnp.split(xbc_c, [D_INNER, D_INNER + SSD_GROUPS * SSD_STATE], axis=-1)
    xs = xs.reshape(bsz, L, SSD_HEADS, SSD_HEAD_DIM)
    b_in = b_in.reshape(bsz, L, SSD_GROUPS, SSD_STATE)
    c_in = c_in.reshape(bsz, L, SSD_GROUPS, SSD_STATE)
    dt = jax.nn.softplus((dt + dt_bias).astype(jnp.float32))
    a = -jnp.exp(a_log.astype(jnp.float32))
    y, new_state = ssd_scan(xs, dt, a, b_in, c_in, ssm_state)
    y = (y + xs * d_skip[:, None]).astype(h.dtype)
    y = y.reshape(bsz, L, D_INNER) * jax.nn.silu(z)
    y = rmsnorm(y.reshape(bsz, L, SSD_GROUPS, D_INNER // SSD_GROUPS),
                norm_w.reshape(SSD_GROUPS, D_INNER // SSD_GROUPS)).reshape(bsz, L, D_INNER)
    return y @ w_out, new_conv, new_state


def partial_rotary(t, pos):
    half = ROT_DIM // 2
    inv_freq = ROPE_THETA ** (-(jnp.arange(half, dtype=jnp.float32) * 2.0) / ROT_DIM)
    ang = pos.astype(jnp.float32)[:, None] * inv_freq
    cos = jnp.cos(ang)[None, :, None, :]
    sin = jnp.sin(ang)[None, :, None, :]
    t1 = t[..., :half].astype(jnp.float32)
    t2 = t[..., half:ROT_DIM].astype(jnp.float32)
    r1 = (t1 * cos - t2 * sin).astype(t.dtype)
    r2 = (t2 * cos + t1 * sin).astype(t.dtype)
    return jnp.concatenate([r1, r2, t[..., ROT_DIM:]], axis=-1)


def moba_attention(q, k, v, q_start):
    bsz, lq, H, Dh = q.shape
    lk = k.shape[1]
    qb = min(MOBA_Q_BLOCK, lq)
    lq_pad = -(-lq // qb) * qb
    nblk = -(-(q_start + lq_pad) // MOBA_BLOCK)
    kpad = [(0, 0), (0, nblk * MOBA_BLOCK - lk), (0, 0), (0, 0)]
    k_blk = jnp.pad(k, kpad).reshape(bsz, nblk, MOBA_BLOCK, H, Dh).transpose(0, 3, 1, 2, 4)
    v_blk = jnp.pad(v, kpad).reshape(bsz, nblk, MOBA_BLOCK, H, Dh).transpose(0, 3, 1, 2, 4)
    n_score = max(nblk, MOBA_TOP_K)
    k_mean = jnp.mean(k_blk.astype(jnp.float32), axis=3)
    k_mean = jnp.pad(k_mean, [(0, 0), (0, 0), (0, n_score - nblk), (0, 0)])
    q = jnp.pad(q, [(0, 0), (0, lq_pad - lq), (0, 0), (0, 0)])
    nsub = lq_pad // qb
    q_sub = q.reshape(bsz, nsub, qb, H, Dh).transpose(1, 0, 3, 2, 4)
    pos_sub = (q_start + jnp.arange(lq_pad)).reshape(nsub, qb)
    scale = Dh ** -0.5
    bi = jnp.arange(bsz)[:, None, None, None]
    hi = jnp.arange(H)[None, :, None, None]

    def attend_block(args):
        qs, ps = args
        own = ps // MOBA_BLOCK
        gate = jnp.einsum('bhqd,bhnd->bhqn', qs.astype(jnp.float32), k_mean)
        fully_past = jnp.arange(n_score)[None, :] < own[:, None]
        gate = jnp.where(fully_past, gate, -jnp.inf)
        _, top = lax.top_k(gate, MOBA_TOP_K)
        sel_valid = top < own[:, None]
        own_idx = jnp.broadcast_to(own[:, None], (bsz, H, qb, 1))
        sel = jnp.concatenate([jnp.minimum(top, nblk - 1), own_idx], axis=-1)
        k_sel = k_blk[bi, hi, sel]
        v_sel = v_blk[bi, hi, sel]
        logits = jnp.einsum('bhqd,bhqskd->bhqsk', qs, k_sel).astype(jnp.float32) * scale
        key_pos = own[:, None] * MOBA_BLOCK + jnp.arange(MOBA_BLOCK)[None, :]
        own_ok = (key_pos <= ps[:, None])[None, None, :, None, :]
        mask = jnp.concatenate([
            jnp.broadcast_to(sel_valid[..., None], (bsz, H, qb, MOBA_TOP_K, MOBA_BLOCK)),
            jnp.broadcast_to(own_ok, (bsz, H, qb, 1, MOBA_BLOCK))], axis=3)
        logits = jnp.where(mask, logits, -jnp.inf)
        p = jax.nn.softmax(logits.reshape(bsz, H, qb, -1), axis=-1).reshape(logits.shape)
        return jnp.einsum('bhqsk,bhqskd->bhqd', p.astype(v_sel.dtype), v_sel)

    out = lax.map(attend_block, (q_sub, pos_sub))
    return out.transpose(1, 0, 3, 2, 4).reshape(bsz, lq_pad, H, Dh)[:, :lq]


def moba_mixer(h, w_qkv, w_o, pos, q_start, k_past, v_past):
    bsz, L, _ = h.shape
    q, k, v = jnp.split(h @ w_qkv, 3, axis=-1)
    q = partial_rotary(q.reshape(bsz, L, ATTN_HEADS, ATTN_HEAD_DIM), pos)
    k = partial_rotary(k.reshape(bsz, L, ATTN_HEADS, ATTN_HEAD_DIM), pos)
    v = v.reshape(bsz, L, ATTN_HEADS, ATTN_HEAD_DIM)
    if k_past is None:
        k_all, v_all = k, v
    else:
        k_all = jnp.concatenate([k_past.astype(k.dtype), k], axis=1)
        v_all = jnp.concatenate([v_past.astype(v.dtype), v], axis=1)
    o = moba_attention(q, k_all, v_all, q_start)
    return o.reshape(bsz, L, D_MODEL) @ w_o, k, v


def conv_ffn(h, w_up, conv_w, conv_b, w_down, past):
    u = h @ w_up
    c, new_past = causal_dwconv(u, conv_w, past)
    a, g = jnp.split(c + conv_b, 2, axis=-1)
    return (jax.nn.silu(g) * a) @ w_down, new_past


def setup_inputs(seed: int = 0) -> dict:
    key = jax.random.key(seed)
    ks = jax.random.split(key, 40)
    f32 = jnp.float32
    nrm = lambda k, shape, s: jax.random.normal(k, shape, f32) * s
    n_pages = PAST_LEN // PAGE_SIZE
    n_pool = (5 * DEC_BATCH * n_pages) // 4
    perm = jax.random.permutation(ks[0], n_pool)
    page_table = perm[:DEC_BATCH * n_pages].reshape(DEC_BATCH, n_pages).astype(jnp.int32)
    dt0 = jnp.exp(jax.random.uniform(ks[1], (N_SSD_LAYERS, SSD_HEADS), f32, math.log(1e-3), math.log(1e-1)))
    return {
        "x_prompt": nrm(ks[2], (BATCH, SEQ, D_MODEL), 1.0),
        "x_sample": nrm(ks[3], (DEC_BATCH, DEC_SEQ, D_MODEL), 1.0),
        "state_sconv": nrm(ks[4], (N_SCONV_LAYERS, DEC_BATCH, SCONV_WIDTH - 1, D_MODEL), 0.5),
        "state_ssm": nrm(ks[5], (N_SSD_LAYERS, DEC_BATCH, SSD_HEADS, SSD_HEAD_DIM, SSD_STATE), 0.1),
        "state_ssm_conv": nrm(ks[6], (N_SSD_LAYERS, DEC_BATCH, SSD_CONV_WIDTH - 1, SSD_CONV_DIM), 0.5),
        "cache_k": nrm(ks[7], (N_ATTN_LAYERS, n_pool, PAGE_SIZE, ATTN_HEADS, ATTN_HEAD_DIM), 1.0),
        "cache_v": nrm(ks[8], (N_ATTN_LAYERS, n_pool, PAGE_SIZE, ATTN_HEADS, ATTN_HEAD_DIM), 1.0),
        "state_ffn_conv": nrm(ks[9], (DEPTH, DEC_BATCH, FFN_CONV_WIDTH - 1, 2 * D_FF), 0.5),
        "page_table": page_table,
        "norm_mix_w": 1.0 + nrm(ks[10], (DEPTH, D_MODEL), 0.02),
        "norm_ffn_w": 1.0 + nrm(ks[11], (DEPTH, D_MODEL), 0.02),
        "norm_final_w": 1.0 + nrm(ks[12], (D_MODEL,), 0.02),
        "sconv_w_in": nrm(ks[13], (N_SCONV_LAYERS, D_MODEL, 3 * D_MODEL), D_MODEL ** -0.5),
        "sconv_conv_w": nrm(ks[14], (N_SCONV_LAYERS, SCONV_WIDTH, D_MODEL), SCONV_WIDTH ** -0.5),
        "sconv_w_out": nrm(ks[15], (N_SCONV_LAYERS, D_MODEL, D_MODEL), D_MODEL ** -0.5),
        "ssd_w_in": nrm(ks[16], (N_SSD_LAYERS, D_MODEL, SSD_IN_DIM), D_MODEL ** -0.5),
        "ssd_conv_w": nrm(ks[17], (N_SSD_LAYERS, SSD_CONV_WIDTH, SSD_CONV_DIM), SSD_CONV_WIDTH ** -0.5),
        "ssd_conv_b": nrm(ks[18], (N_SSD_LAYERS, SSD_CONV_DIM), 0.02),
        "ssd_dt_bias": dt0 + jnp.log(-jnp.expm1(-dt0)),
        "ssd_a_log": jnp.log(jax.random.uniform(ks[19], (N_SSD_LAYERS, SSD_HEADS), f32, 1.0, 16.0)),
        "ssd_d": 1.0 + nrm(ks[20], (N_SSD_LAYERS, SSD_HEADS), 0.02),
        "ssd_norm_w": 1.0 + nrm(ks[21], (N_SSD_LAYERS, D_INNER), 0.02),
        "ssd_w_out": nrm(ks[22], (N_SSD_LAYERS, D_INNER, D_MODEL), D_INNER ** -0.5),
        "attn_w_qkv": nrm(ks[23], (N_ATTN_LAYERS, D_MODEL, 3 * D_MODEL), D_MODEL ** -0.5),
        "attn_w_o": nrm(ks[24], (N_ATTN_LAYERS, D_MODEL, D_MODEL), D_MODEL ** -0.5),
        "ffn_w_up": nrm(ks[25], (DEPTH, D_MODEL, 2 * D_FF), D_MODEL ** -0.5),
        "ffn_conv_w": nrm(ks[26], (DEPTH, FFN_CONV_WIDTH, 2 * D_FF), FFN_CONV_WIDTH ** -0.5),
        "ffn_conv_b": nrm(ks[27], (DEPTH, 2 * D_FF), 0.02),
        "ffn_w_down": nrm(ks[28], (DEPTH, D_FF, D_MODEL), D_FF ** -0.5),
    }


def reference(x_prompt, x_sample, state_sconv, state_ssm, state_ssm_conv, cache_k, cache_v,
              state_ffn_conv, page_table, norm_mix_w, norm_ffn_w, norm_final_w,
              sconv_w_in, sconv_conv_w, sconv_w_out,
              ssd_w_in, ssd_conv_w, ssd_conv_b, ssd_dt_bias, ssd_a_log, ssd_d, ssd_norm_w, ssd_w_out,
              attn_w_qkv, attn_w_o, ffn_w_up, ffn_conv_w, ffn_conv_b, ffn_w_down):
    n_pages = PAST_LEN // PAGE_SIZE
    pos_p = jnp.arange(SEQ)
    pos_s = PAST_LEN + jnp.arange(DEC_SEQ)
    xp, xs = x_prompt, x_sample
    dt = xp.dtype
    sconv_p, sconv_s, ssm_p, ssm_s, ssmc_p, ssmc_s = [], [], [], [], [], []
    k_p, v_p, k_s, v_s, ffnc_p, ffnc_s = [], [], [], [], [], []
    for i in range(DEPTH):
        kind, j = i % N_MIXERS, i // N_MIXERS
        hp = rmsnorm(xp, norm_mix_w[i])
        hs = rmsnorm(xs, norm_mix_w[i])
        if kind == 0:
            zero = jnp.zeros((BATCH, SCONV_WIDTH - 1, D_MODEL), dt)
            op, st_p = short_conv_mixer(hp, sconv_w_in[j], sconv_conv_w[j], sconv_w_out[j], zero)
            os_, st_s = short_conv_mixer(hs, sconv_w_in[j], sconv_conv_w[j], sconv_w_out[j], state_sconv[j])
            sconv_p.append(st_p)
            sconv_s.append(st_s)
        elif kind == 1:
            zc = jnp.zeros((BATCH, SSD_CONV_WIDTH - 1, SSD_CONV_DIM), dt)
            zs = jnp.zeros((BATCH, SSD_HEADS, SSD_HEAD_DIM, SSD_STATE), jnp.float32)
            args = (ssd_w_in[j], ssd_conv_w[j], ssd_conv_b[j], ssd_dt_bias[j], ssd_a_log[j], ssd_d[j],
                    ssd_norm_w[j], ssd_w_out[j])
            op, cp, sp = ssd_mixer(hp, *args, zs, zc)
            os_, cs, ss = ssd_mixer(hs, *args, state_ssm[j], state_ssm_conv[j])
            ssmc_p.append(cp)
            ssmc_s.append(cs)
            ssm_p.append(sp)
            ssm_s.append(ss)
        else:
            kp_past = cache_k[j][page_table].reshape(DEC_BATCH, n_pages * PAGE_SIZE, ATTN_HEADS, ATTN_HEAD_DIM)
            vp_past = cache_v[j][page_table].reshape(DEC_BATCH, n_pages * PAGE_SIZE, ATTN_HEADS, ATTN_HEAD_DIM)
            op, kp, vp = moba_mixer(hp, attn_w_qkv[j], attn_w_o[j], pos_p, 0, None, None)
            os_, ks_, vs_ = moba_mixer(hs, attn_w_qkv[j], attn_w_o[j], pos_s, PAST_LEN, kp_past, vp_past)
            k_p.append(kp)
            v_p.append(vp)
            k_s.append(ks_)
            v_s.append(vs_)
        xp = xp + op
        xs = xs + os_
        hp = rmsnorm(xp, norm_ffn_w[i])
        hs = rmsnorm(xs, norm_ffn_w[i])
        zf = jnp.zeros((BATCH, FFN_CONV_WIDTH - 1, 2 * D_FF), dt)
        fp, fcp = conv_ffn(hp, ffn_w_up[i], ffn_conv_w[i], ffn_conv_b[i], ffn_w_down[i], zf)
        fs, fcs = conv_ffn(hs, ffn_w_up[i], ffn_conv_w[i], ffn_conv_b[i], ffn_w_down[i], state_ffn_conv[i])
        ffnc_p.append(fcp)
        ffnc_s.append(fcs)
        xp = xp + fp
        xs = xs + fs
    y_prompt = rmsnorm(xp, norm_final_w)
    y_sample = rmsnorm(xs, norm_final_w)
    return (y_prompt, y_sample,
            jnp.stack(sconv_p), jnp.stack(sconv_s),
            jnp.stack(ssm_p), jnp.stack(ssm_s),
            jnp.stack(ssmc_p), jnp.stack(ssmc_s),
            jnp.stack(k_p), jnp.stack(v_p), jnp.stack(k_s), jnp.stack(v_s),
            jnp.stack(ffnc_p), jnp.stack(ffnc_s))
```

```python
import functools

import jax
import jax.numpy as jnp
from jax import lax
from jax.experimental import pallas as pl
from jax.experimental.pallas import tpu as pltpu

F32 = jnp.float32
BF16 = jnp.bfloat16

D_MODEL = 1024
BATCH = 8
SEQ = 2048
DEPTH = 4
DEC_BATCH = 128
PAST_LEN = 2048
PAGE_SIZE = 128
N_PAGES = PAST_LEN // PAGE_SIZE
N_MIXERS = 3
RMS_EPS = 1e-6
SCONV_WIDTH = 3
D_INNER = 2 * D_MODEL
SSD_HEAD_DIM = 64
SSD_HEADS = D_INNER // SSD_HEAD_DIM
SSD_GROUPS = 4
SSD_STATE = 128
SSD_CONV_WIDTH = 4
SSD_CONV_DIM = D_INNER + 2 * SSD_GROUPS * SSD_STATE
SSD_CHUNK = 128
SSD_PAIRS = SSD_HEADS // 2
ATTN_HEAD_DIM = 64
ATTN_HEADS = D_MODEL // ATTN_HEAD_DIM
MOBA_BLOCK = 256
MOBA_TOP_K = 3
ROPE_THETA = 500000.0
ROT_DIM = ATTN_HEAD_DIM // 4
D_FF = 2816
FFN_CONV_WIDTH = 3

LANES = 128
SUBLANES = 8
VMEM_LIMIT = 56 * 1024 * 1024
N_TOK = BATCH * SEQ
NEG_INF = float("-inf")


def _params(*sem):
    return pltpu.CompilerParams(dimension_semantics=sem, vmem_limit_bytes=VMEM_LIMIT)


def _bdot(a, b):
    return jnp.dot(a.astype(BF16), b.astype(BF16), preferred_element_type=F32)


def _bdot_nt(a, b):
    return lax.dot_general(a.astype(BF16), b.astype(BF16), (((1,), (1,)), ((), ())),
                           preferred_element_type=F32)


def _dot3_nt(a, b):
    a_hi = a.astype(BF16)
    b_hi = b.astype(BF16)
    a_lo = (a - a_hi.astype(F32)).astype(BF16)
    b_lo = (b - b_hi.astype(F32)).astype(BF16)
    dn = (((1,), (1,)), ((), ()))
    return (lax.dot_general(a_hi, b_hi, dn, preferred_element_type=F32)
            + lax.dot_general(a_hi, b_lo, dn, preferred_element_type=F32)
            + lax.dot_general(a_lo, b_hi, dn, preferred_element_type=F32))


def _rms(x, w):
    inv = lax.rsqrt(jnp.mean(x * x, axis=-1, keepdims=True) + RMS_EPS)
    return (x * inv) * w


def _silu(x):
    return x * (1.0 / (1.0 + jnp.exp(-x)))


def _softplus(x):
    return jnp.maximum(x, 0.0) + jnp.log1p(jnp.exp(-jnp.abs(x)))


def _conv_rows(u, w, prev):
    width = w.shape[0]
    row8 = lax.broadcasted_iota(jnp.int32, (SUBLANES, u.shape[1]), 0)
    out = u * w[width - 1:width, :]
    for k in range(1, width):
        sh = pltpu.roll(u, k, axis=0)
        head = sh[:SUBLANES]
        for t in range(k):
            src = SUBLANES - k + t
            head = jnp.where(row8 == t, prev[src:src + 1, :], head)
        sh = jnp.concatenate([head, sh[SUBLANES:]], axis=0)
        out = out + sh * w[width - 1 - k:width - k, :]
    return out


def _conv_step(u, w, past):
    width = w.shape[0]
    out = u * w[width - 1:width, :]
    for k in range(width - 1):
        out = out + past[k] * w[k:k + 1, :]
    return out


SCONV_TL = 512


def _sconv_prompt_body(x_ref, nw_ref, win_ref, cw_ref, wout_ref, o_ref, st_ref, carry_ref):
    l = pl.program_id(1)

    @pl.when(l == 0)
    def _():
        carry_ref[...] = jnp.zeros_like(carry_ref)

    x = x_ref[...]
    h = _rms(x, nw_ref[...]).astype(BF16)
    p = jnp.dot(h, win_ref[...], preferred_element_type=F32)
    g = p[:, D_MODEL:2 * D_MODEL] * p[:, 2 * D_MODEL:]
    c = _conv_rows(g, cw_ref[...], carry_ref[...])
    carry_ref[...] = g[SCONV_TL - SUBLANES:, :]
    o_ref[...] = x + _bdot(p[:, :D_MODEL] * c, wout_ref[...])

    @pl.when(l == pl.num_programs(1) - 1)
    def _():
        st_ref[0] = g[SCONV_TL - (SCONV_WIDTH - 1):, :]


def _sconv_prompt(x, nw, w_in, conv_w, w_out):
    nl = SEQ // SCONV_TL
    row = lambda b, l: (b * nl + l, 0)
    const = lambda b, l: (0, 0)
    return pl.pallas_call(
        _sconv_prompt_body,
        grid=(BATCH, nl),
        in_specs=[
            pl.BlockSpec((SCONV_TL, D_MODEL), row),
            pl.BlockSpec((1, D_MODEL), const),
            pl.BlockSpec((D_MODEL, 3 * D_MODEL), const),
            pl.BlockSpec((SCONV_WIDTH, D_MODEL), const),
            pl.BlockSpec((D_MODEL, D_MODEL), const),
        ],
        out_specs=[
            pl.BlockSpec((SCONV_TL, D_MODEL), row),
            pl.BlockSpec((1, SCONV_WIDTH - 1, D_MODEL), lambda b, l: (b, 0, 0)),
        ],
        out_shape=[
            jax.ShapeDtypeStruct((N_TOK, D_MODEL), F32),
            jax.ShapeDtypeStruct((BATCH, SCONV_WIDTH - 1, D_MODEL), F32),
        ],
        scratch_shapes=[pltpu.VMEM((SUBLANES, D_MODEL), F32)],
        compiler_params=_params("arbitrary", "arbitrary"),
        name="sconv_prompt",
    )(x, nw, w_in, conv_w, w_out)


FFN_TL = 512
FFN_CHUNK = D_FF // 2


def _ffn_prompt_body(x_ref, nw_ref, wup_ref, cw_ref, cb_ref, wdn_ref, fnw_ref, o_ref, st_ref,
                     carry_ref, *, final):
    l = pl.program_id(1)

    @pl.when(l == 0)
    def _():
        carry_ref[...] = jnp.zeros_like(carry_ref)

    x = x_ref[...]
    h = _rms(x, nw_ref[...]).astype(BF16)
    acc = x
    last = l == pl.num_programs(1) - 1
    for c in range(D_FF // FFN_CHUNK):
        halves = []
        for base in (0, D_FF):
            lo = base + c * FFN_CHUNK
            hi = lo + FFN_CHUNK
            u = jnp.dot(h, wup_ref[:, lo:hi], preferred_element_type=F32)
            conv = _conv_rows(u, cw_ref[:, lo:hi], carry_ref[:, lo:hi]) + cb_ref[:, lo:hi]
            carry_ref[:, lo:hi] = u[FFN_TL - SUBLANES:, :]

            @pl.when(last)
            def _(u=u, lo=lo, hi=hi):
                st_ref[0, :, lo:hi] = u[FFN_TL - (FFN_CONV_WIDTH - 1):, :]

            halves.append(conv)
        a, g = halves
        acc = acc + _bdot(_silu(g) * a, wdn_ref[c * FFN_CHUNK:(c + 1) * FFN_CHUNK, :])
    if final:
        acc = _rms(acc, fnw_ref[...])
    o_ref[...] = acc


def _ffn_prompt(x, nw, w_up, conv_w, conv_b, w_down, fnw, final):
    nl = SEQ // FFN_TL
    row = lambda b, l: (b * nl + l, 0)
    const = lambda b, l: (0, 0)
    once = pl.Buffered(1)
    return pl.pallas_call(
        functools.partial(_ffn_prompt_body, final=final),
        grid=(BATCH, nl),
        in_specs=[
            pl.BlockSpec((FFN_TL, D_MODEL), row),
            pl.BlockSpec((1, D_MODEL), const),
            pl.BlockSpec((D_MODEL, 2 * D_FF), const, pipeline_mode=once),
            pl.BlockSpec((FFN_CONV_WIDTH, 2 * D_FF), const),
            pl.BlockSpec((1, 2 * D_FF), const),
            pl.BlockSpec((D_FF, D_MODEL), const, pipeline_mode=once),
            pl.BlockSpec((1, D_MODEL), const),
        ],
        out_specs=[
            pl.BlockSpec((FFN_TL, D_MODEL), row),
            pl.BlockSpec((1, FFN_CONV_WIDTH - 1, 2 * D_FF), lambda b, l: (b, 0, 0)),
        ],
        out_shape=[
            jax.ShapeDtypeStruct((N_TOK, D_MODEL), F32),
            jax.ShapeDtypeStruct((BATCH, FFN_CONV_WIDTH - 1, 2 * D_FF), F32),
        ],
        scratch_shapes=[pltpu.VMEM((SUBLANES, 2 * D_FF), F32)],
        compiler_params=_params("arbitrary", "arbitrary"),
        name="ffn_prompt",
    )(x, nw, w_up, conv_w, conv_b, w_down, fnw)


SSD_TL = 256


def _ssd_inproj_prompt_body(x_ref, nw_ref, wz_ref, wx_ref, wdt_ref, cw_ref, cb_ref, dtb_ref,
                            z_ref, xbc_ref, dt_ref, st_ref, carry_ref):
    l = pl.program_id(1)

    @pl.when(l == 0)
    def _():
        carry_ref[...] = jnp.zeros_like(carry_ref)

    h = _rms(x_ref[...], nw_ref[...]).astype(BF16)
    z_ref[...] = jnp.dot(h, wz_ref[...], preferred_element_type=F32)
    dt_ref[...] = _softplus(jnp.dot(h, wdt_ref[...], preferred_element_type=F32) + dtb_ref[...])
    u = jnp.dot(h, wx_ref[...], preferred_element_type=F32)
    xbc_ref[...] = _silu(_conv_rows(u, cw_ref[...], carry_ref[...]) + cb_ref[...])
    carry_ref[...] = u[SSD_TL - SUBLANES:, :]

    @pl.when(l == pl.num_programs(1) - 1)
    def _():
        st_ref[0] = u[SSD_TL - (SSD_CONV_WIDTH - 1):, :]


def _ssd_inproj_prompt(x, nw, w_z, w_x, w_dt, conv_w, conv_b, dt_bias):
    nl = SEQ // SSD_TL
    row = lambda b, l: (b * nl + l, 0)
    const = lambda b, l: (0, 0)
    return pl.pallas_call(
        _ssd_inproj_prompt_body,
        grid=(BATCH, nl),
        in_specs=[
            pl.BlockSpec((SSD_TL, D_MODEL), row),
            pl.BlockSpec((1, D_MODEL), const),
            pl.BlockSpec((D_MODEL, D_INNER), const),
            pl.BlockSpec((D_MODEL, SSD_CONV_DIM), const),
            pl.BlockSpec((D_MODEL, LANES), const),
            pl.BlockSpec((SSD_CONV_WIDTH, SSD_CONV_DIM), const),
            pl.BlockSpec((1, SSD_CONV_DIM), const),
            pl.BlockSpec((1, LANES), const),
        ],
        out_specs=[
            pl.BlockSpec((SSD_TL, D_INNER), row),
            pl.BlockSpec((SSD_TL, SSD_CONV_DIM), row),
            pl.BlockSpec((SSD_TL, LANES), row),
            pl.BlockSpec((1, SSD_CONV_WIDTH - 1, SSD_CONV_DIM), lambda b, l: (b, 0, 0)),
        ],
        out_shape=[
            jax.ShapeDtypeStruct((N_TOK, D_INNER), F32),
            jax.ShapeDtypeStruct((N_TOK, SSD_CONV_DIM), F32),
            jax.ShapeDtypeStruct((N_TOK, LANES), F32),
            jax.ShapeDtypeStruct((BATCH, SSD_CONV_WIDTH - 1, SSD_CONV_DIM), F32),
        ],
        scratch_shapes=[pltpu.VMEM((SUBLANES, SSD_CONV_DIM), F32)],
        compiler_params=_params("arbitrary", "arbitrary"),
        name="ssd_inproj_prompt",
    )(x, nw, w_z, w_x, w_dt, conv_w, conv_b, dt_bias)


def _cumsum_rows(x):
    n = x.shape[0]
    row = lax.broadcasted_iota(jnp.int32, x.shape, 0)
    k = 1
    while k < n:
        x = x + jnp.where(row >= k, pltpu.roll(x, k, axis=0), 0.0)
        k *= 2
    return x


def _pair_lanes(vals, h0, shape):
    lane = lax.broadcasted_iota(jnp.int32, shape, 1)
    return jnp.where(lane < SSD_HEAD_DIM, vals[:, h0:h0 + 1], vals[:, h0 + 1:h0 + 2])


def _group_norm_gate(y, z, nw):
    y = y * _silu(z)
    gw = D_INNER // SSD_GROUPS
    outs = []
    for g in range(SSD_GROUPS):
        yg = y[:, g * gw:(g + 1) * gw]
        outs.append(_rms(yg, nw[:, g * gw:(g + 1) * gw]))
    return jnp.concatenate(outs, axis=1)


def _ssd_scan_prompt_body(x_ref, z_ref, xbc_ref, dt_ref, alog_ref, dskip_ref, nw_ref, wout_ref,
                          o_ref, st_ref, y_ref):
    c = pl.program_id(1)

    @pl.when(c == 0)
    def _():
        st_ref[...] = jnp.zeros_like(st_ref)

    q = SSD_CHUNK
    dt = dt_ref[...]
    acum = _cumsum_rows(dt * (-jnp.exp(alog_ref[...])))
    acum_t = acum.T
    e_acum = jnp.exp(acum)
    last = acum[q - 1:q, :]
    e_end = jnp.exp(last - acum)
    e_chunk = jnp.exp(last)
    row = lax.broadcasted_iota(jnp.int32, (q, q), 0)
    col = lax.broadcasted_iota(jnp.int32, (q, q), 1)
    causal = row >= col
    lane_lo = col < SSD_HEAD_DIM
    row_lo = row < SSD_HEAD_DIM

    for g in range(SSD_GROUPS):
        b_g = xbc_ref[:, D_INNER + g * SSD_STATE:D_INNER + (g + 1) * SSD_STATE].astype(BF16)
        c_g = xbc_ref[:, D_INNER + (SSD_GROUPS + g) * SSD_STATE:
                      D_INNER + (SSD_GROUPS + g + 1) * SSD_STATE].astype(BF16)
        cb = _bdot_nt(c_g, b_g)
        for j in range(g * SSD_PAIRS // SSD_GROUPS, (g + 1) * SSD_PAIRS // SSD_GROUPS):
            h0 = 2 * j
            xs = xbc_ref[:, j * LANES:(j + 1) * LANES]
            xdt = xs * _pair_lanes(dt, h0, (q, LANES))
            xdt_b = xdt.astype(BF16)
            yd = []
            for h in (h0, h0 + 1):
                seg = acum[:, h:h + 1] - acum_t[h:h + 1, :]
                decay = jnp.exp(jnp.where(causal, seg, NEG_INF))
                yd.append(_bdot(cb * decay, xdt_b))
            y = jnp.where(lane_lo, yd[0], yd[1])
            state = st_ref[0, j]
            y = y + _bdot_nt(c_g, state) * _pair_lanes(e_acum, h0, (q, LANES))
            y = y + xs * dskip_ref[:, j * LANES:(j + 1) * LANES]
            y_ref[:, j * LANES:(j + 1) * LANES] = y
            xw = xdt * _pair_lanes(e_end, h0, (q, LANES))
            upd = _bdot(xw.T, b_g)
            scale = jnp.where(row_lo, e_chunk[:, h0:h0 + 1], e_chunk[:, h0 + 1:h0 + 2])
            st_ref[0, j] = state * scale + upd

    yn = _group_norm_gate(y_ref[...], z_ref[...], nw_ref[...])
    o_ref[...] = x_ref[...] + _bdot(yn, wout_ref[...])


def _ssd_scan_prompt(x, z, xbc, dt, a_log, d_skip, nw, w_out):
    nc = SEQ // SSD_CHUNK
    row = lambda b, c: (b * nc + c, 0)
    const = lambda b, c: (0, 0)
    return pl.pallas_call(
        _ssd_scan_prompt_body,
        grid=(BATCH, nc),
        in_specs=[
            pl.BlockSpec((SSD_CHUNK, D_MODEL), row),
            pl.BlockSpec((SSD_CHUNK, D_INNER), row),
            pl.BlockSpec((SSD_CHUNK, SSD_CONV_DIM), row),
            pl.BlockSpec((SSD_CHUNK, LANES), row),
            pl.BlockSpec((1, LANES), const),
            pl.BlockSpec((1, D_INNER), const),
            pl.BlockSpec((1, D_INNER), const),
            pl.BlockSpec((D_INNER, D_MODEL), const),
        ],
        out_specs=[
            pl.BlockSpec((SSD_CHUNK, D_MODEL), row),
            pl.BlockSpec((1, SSD_PAIRS, 2 * SSD_HEAD_DIM, SSD_STATE), lambda b, c: (b, 0, 0, 0)),
        ],
        out_shape=[
            jax.ShapeDtypeStruct((N_TOK, D_MODEL), F32),
            jax.ShapeDtypeStruct((BATCH, SSD_PAIRS, 2 * SSD_HEAD_DIM, SSD_STATE), F32),
        ],
        scratch_shapes=[pltpu.VMEM((SSD_CHUNK, D_INNER), F32)],
        compiler_params=_params("arbitrary", "arbitrary"),
        name="ssd_scan_prompt",
    )(x, z, xbc, dt, a_log, d_skip, nw, w_out)


QKV_TL = 512


def _rope(t, cos, sin_dn, sin_up):
    reps = t.shape[1] // LANES
    half = ROT_DIM // 2
    tile = lambda a: jnp.concatenate([a] * reps, axis=1)
    return (t * tile(cos)
            + pltpu.roll(t, t.shape[1] - half, axis=1) * tile(sin_dn)
            + pltpu.roll(t, half, axis=1) * tile(sin_up))


def _qkv_prompt_body(x_ref, nw_ref, w_ref, cos_ref, sdn_ref, sup_ref, q_ref, k_ref, v_ref):
    h = _rms(x_ref[...], nw_ref[...]).astype(BF16)
    cos, sdn, sup = cos_ref[...], sdn_ref[...], sup_ref[...]
    q = jnp.dot(h, w_ref[:, :D_MODEL], preferred_element_type=F32)
    q_ref[...] = _rope(q, cos, sdn, sup)
    k = jnp.dot(h, w_ref[:, D_MODEL:2 * D_MODEL], preferred_element_type=F32)
    k_ref[...] = _rope(k, cos, sdn, sup)
    v_ref[...] = jnp.dot(h, w_ref[:, 2 * D_MODEL:], preferred_element_type=F32)


def _qkv_prompt(x, nw, w_qkv, cos, sdn, sup):
    nl = SEQ // QKV_TL
    row = lambda b, l: (b * nl + l, 0)
    const = lambda b, l: (0, 0)
    tab = pl.BlockSpec((QKV_TL, LANES), lambda b, l: (l, 0))
    out = jax.ShapeDtypeStruct((N_TOK, D_MODEL), F32)
    return pl.pallas_call(
        _qkv_prompt_body,
        grid=(BATCH, nl),
        in_specs=[
            pl.BlockSpec((QKV_TL, D_MODEL), row),
            pl.BlockSpec((1, D_MODEL), const),
            pl.BlockSpec((D_MODEL, 3 * D_MODEL), const),
            tab, tab, tab,
        ],
        out_specs=[pl.BlockSpec((QKV_TL, D_MODEL), row)] * 3,
        out_shape=[out, out, out],
        compiler_params=_params("arbitrary", "arbitrary"),
        name="qkv_prompt",
    )(x, nw, w_qkv, cos, sdn, sup)


N_BLK = SEQ // MOBA_BLOCK
ATTN_SCALE = ATTN_HEAD_DIM ** -0.5


def _moba_select(gate, n, own):
    lane = lax.broadcasted_iota(jnp.int32, gate.shape, 1)
    g_n = jnp.sum(jnp.where(lane == n, gate, 0.0), axis=1, keepdims=True)
    ahead = (gate > g_n) | ((gate == g_n) & (lane < n))
    rank = jnp.sum(jnp.where(ahead & (lane < own), 1.0, 0.0), axis=1, keepdims=True)
    return rank < MOBA_TOP_K


def _moba_prompt_body(q_ref, k_ref, v_ref, o_ref, kmean_ref):
    i = pl.program_id(2)
    blk = MOBA_BLOCK
    dh = ATTN_HEAD_DIM

    @pl.when(i == 0)
    def _():
        kmean_ref[...] = jnp.zeros_like(kmean_ref)
        for n in range(N_BLK):
            kmean_ref[n:n + 1, :] = jnp.mean(k_ref[n * blk:(n + 1) * blk, :], axis=0, keepdims=True)

    own0 = pl.multiple_of(i * blk, blk)
    row = lax.broadcasted_iota(jnp.int32, (blk, blk), 0)
    col = lax.broadcasted_iota(jnp.int32, (blk, blk), 1)
    outs = []
    for hh in range(2):
        lo, hi = hh * dh, (hh + 1) * dh
        q = q_ref[:, lo:hi]
        qb = q.astype(BF16)
        gate = _dot3_nt(q, kmean_ref[:, lo:hi])
        s = _bdot_nt(qb, k_ref[pl.ds(own0, blk), lo:hi]) * ATTN_SCALE
        s = jnp.where(col <= row, s, NEG_INF)
        m = jnp.max(s, axis=1, keepdims=True)
        p = jnp.exp(s - m)
        l = jnp.sum(p, axis=1, keepdims=True)
        acc = _bdot(p, v_ref[pl.ds(own0, blk), lo:hi])

        def past_block(n, carry, qb=qb, gate=gate, lo=lo, hi=hi):
            m, l, acc = carry
            k0 = pl.multiple_of(n * blk, blk)
            s = _bdot_nt(qb, k_ref[pl.ds(k0, blk), lo:hi]) * ATTN_SCALE
            s = jnp.where(_moba_select(gate, n, i), s, NEG_INF)
            m_new = jnp.maximum(m, jnp.max(s, axis=1, keepdims=True))
            alpha = jnp.exp(m - m_new)
            p = jnp.exp(s - m_new)
            l = l * alpha + jnp.sum(p, axis=1, keepdims=True)
            acc = acc * alpha + _bdot(p, v_ref[pl.ds(k0, blk), lo:hi])
            return m_new, l, acc

        m, l, acc = lax.fori_loop(0, i, past_block, (m, l, acc))
        outs.append(acc / l)
    o_ref[...] = jnp.concatenate(outs, axis=1)


def _moba_prompt(q, k, v):
    pairs = ATTN_HEADS // 2
    qspec = pl.BlockSpec((MOBA_BLOCK, LANES), lambda b, p, i: (b * N_BLK + i, p))
    kvspec = pl.BlockSpec((SEQ, LANES), lambda b, p, i: (b, p))
    return pl.pallas_call(
        _moba_prompt_body,
        grid=(BATCH, pairs, N_BLK),
        in_specs=[qspec, kvspec, kvspec],
        out_specs=qspec,
        out_shape=jax.ShapeDtypeStruct((N_TOK, D_MODEL), F32),
        scratch_shapes=[pltpu.VMEM((LANES, LANES), F32)],
        compiler_params=_params("arbitrary", "arbitrary", "arbitrary"),
        name="moba_prompt",
    )(q, k, v)


PROJ_TL = 512


def _proj_residual_body(a_ref, w_ref, x_ref, o_ref):
    o_ref[...] = x_ref[...] + _bdot(a_ref[...], w_ref[...])


def _proj_residual(a, w, x, tl):
    n, kdim = a.shape
    row = lambda r: (r, 0)
    return pl.pallas_call(
        _proj_residual_body,
        grid=(n // tl,),
        in_specs=[
            pl.BlockSpec((tl, kdim), row),
            pl.BlockSpec(w.shape, lambda r: (0, 0)),
            pl.BlockSpec((tl, w.shape[1]), row),
        ],
        out_specs=pl.BlockSpec((tl, w.shape[1]), row),
        out_shape=jax.ShapeDtypeStruct((n, w.shape[1]), F32),
        compiler_params=_params("arbitrary"),
        name="proj_residual",
    )(a, w, x)


def _whole(shape):
    return pl.BlockSpec(shape, lambda: (0,) * len(shape))


def _call_whole(body, name, out_shapes, *args):
    return pl.pallas_call(
        body,
        in_specs=[_whole(a.shape) for a in args],
        out_specs=[_whole(s.shape) for s in out_shapes],
        out_shape=out_shapes,
        compiler_params=pltpu.CompilerParams(vmem_limit_bytes=VMEM_LIMIT),
        name=name,
    )(*args)


def _sconv_sample_body(x_ref, nw_ref, win_ref, cw_ref, wout_ref, p0_ref, p1_ref, o_ref, g_ref):
    x = x_ref[...]
    h = _rms(x, nw_ref[...]).astype(BF16)
    p = jnp.dot(h, win_ref[...], preferred_element_type=F32)
    g = p[:, D_MODEL:2 * D_MODEL] * p[:, 2 * D_MODEL:]
    c = _conv_step(g, cw_ref[...], [p0_ref[...], p1_ref[...]])
    o_ref[...] = x + _bdot(p[:, :D_MODEL] * c, wout_ref[...])
    g_ref[...] = g


def _sconv_sample(x, nw, w_in, conv_w, w_out, past):
    sds = jax.ShapeDtypeStruct((DEC_BATCH, D_MODEL), F32)
    x_new, g = _call_whole(_sconv_sample_body, "sconv_sample", [sds, sds],
                           x, nw, w_in, conv_w, w_out, past[:, 0], past[:, 1])
    return x_new, jnp.stack([past[:, 1], g], axis=1)


def _ffn_sample_body(x_ref, nw_ref, wup_ref, cw_ref, cb_ref, wdn_ref, fnw_ref, p0_ref, p1_ref,
                     o_ref, u_ref, *, final):
    x = x_ref[...]
    h = _rms(x, nw_ref[...]).astype(BF16)
    acc = x
    for c in range(D_FF // FFN_CHUNK):
        halves = []
        for base in (0, D_FF):
            lo = base + c * FFN_CHUNK
            hi = lo + FFN_CHUNK
            u = jnp.dot(h, wup_ref[:, lo:hi], preferred_element_type=F32)
            u_ref[:, lo:hi] = u
            halves.append(_conv_step(u, cw_ref[:, lo:hi], [p0_ref[:, lo:hi], p1_ref[:, lo:hi]])
                          + cb_ref[:, lo:hi])
        a, g = halves
        acc = acc + _bdot(_silu(g) * a, wdn_ref[c * FFN_CHUNK:(c + 1) * FFN_CHUNK, :])
    if final:
        acc = _rms(acc, fnw_ref[...])
    o_ref[...] = acc


def _ffn_sample(x, nw, w_up, conv_w, conv_b, w_down, fnw, past, final):
    outs = [jax.ShapeDtypeStruct((DEC_BATCH, D_MODEL), F32),
            jax.ShapeDtypeStruct((DEC_BATCH, 2 * D_FF), F32)]
    x_new, u = _call_whole(functools.partial(_ffn_sample_body, final=final), "ffn_sample", outs,
                           x, nw, w_up, conv_w, conv_b, w_down, fnw, past[:, 0], past[:, 1])
    return x_new, jnp.stack([past[:, 1], u], axis=1)


def _ssd_inproj_sample_body(x_ref, nw_ref, wz_ref, wx_ref, wdt_ref, cw_ref, cb_ref, dtb_ref,
                            p0_ref, p1_ref, p2_ref, z_ref, xbc_ref, dt_ref, u_ref):
    h = _rms(x_ref[...], nw_ref[...]).astype(BF16)
    z_ref[...] = jnp.dot(h, wz_ref[...], preferred_element_type=F32)
    dt_ref[...] = _softplus(jnp.dot(h, wdt_ref[...], preferred_element_type=F32) + dtb_ref[...])
    u = jnp.dot(h, wx_ref[...], preferred_element_type=F32)
    u_ref[...] = u
    conv = _conv_step(u, cw_ref[...], [p0_ref[...], p1_ref[...], p2_ref[...]])
    xbc_ref[...] = _silu(conv + cb_ref[...])


def _ssd_state_sample_body(xbc_ref, dt_ref, alog_ref, dskip_ref, st_ref, y_ref, nst_ref):
    n = SSD_STATE
    xbc = xbc_ref[0]
    dt = dt_ref[0]
    e_dec = jnp.exp(dt * (-jnp.exp(alog_ref[...])))
    row = lax.broadcasted_iota(jnp.int32, (n, n), 0)
    col = lax.broadcasted_iota(jnp.int32, (n, n), 1)
    row_lo = row < SSD_HEAD_DIM
    for j in range(SSD_PAIRS):
        g = j // (SSD_PAIRS // SSD_GROUPS)
        h0 = 2 * j
        b_row = xbc[:, D_INNER + g * n:D_INNER + (g + 1) * n]
        c_row = xbc[:, D_INNER + (SSD_GROUPS + g) * n:D_INNER + (SSD_GROUPS + g + 1) * n]
        xs = xbc[:, j * LANES:(j + 1) * LANES]
        xdt = xs * _pair_lanes(dt, h0, (1, LANES))
        state = st_ref[0, j]
        y = jnp.sum(c_row * b_row, axis=1, keepdims=True) * xdt
        c_rows = jnp.broadcast_to(c_row, (SUBLANES, n))
        y = y + _bdot_nt(c_rows, state)[0:1, :] * _pair_lanes(e_dec, h0, (1, LANES))
        y_ref[0, :, j * LANES:(j + 1) * LANES] = y + xs * dskip_ref[:, j * LANES:(j + 1) * LANES]
        diag = jnp.where(row == col, jnp.broadcast_to(xdt, (n, n)), 0.0)
        upd = _bdot(diag, jnp.broadcast_to(b_row, (n, n)))
        scale = jnp.where(row_lo, e_dec[:, h0:h0 + 1], e_dec[:, h0 + 1:h0 + 2])
        nst_ref[0, j] = state * scale + upd


def _ssd_out_sample_body(x_ref, y_ref, z_ref, nw_ref, wout_ref, o_ref):
    yn = _group_norm_gate(y_ref[...], z_ref[...], nw_ref[...])
    o_ref[...] = x_ref[...] + _bdot(yn, wout_ref[...])


def _ssd_sample(x, nw, w_z, w_x, w_dt, conv_w, conv_b, dt_bias, a_log, d_skip, norm_w, w_out,
                state, conv_past):
    r = DEC_BATCH
    outs = [jax.ShapeDtypeStruct((r, D_INNER), F32), jax.ShapeDtypeStruct((r, SSD_CONV_DIM), F32),
            jax.ShapeDtypeStruct((r, LANES), F32), jax.ShapeDtypeStruct((r, SSD_CONV_DIM), F32)]
    z, xbc, dt, u = _call_whole(_ssd_inproj_sample_body, "ssd_inproj_sample", outs,
                                x, nw, w_z, w_x, w_dt, conv_w, conv_b, dt_bias,
                                conv_past[:, 0], conv_past[:, 1], conv_past[:, 2])
    new_conv = jnp.concatenate([conv_past[:, 1:], u[:, None]], axis=1)
    st_shape = (r, SSD_PAIRS, 2 * SSD_HEAD_DIM, SSD_STATE)
    st_spec = pl.BlockSpec((1,) + st_shape[1:], lambda b: (b, 0, 0, 0))
    vec = lambda w: pl.BlockSpec((1, 1, w), lambda b: (b, 0, 0))
    const = lambda w: pl.BlockSpec((1, w), lambda b: (0, 0))
    y, new_state = pl.pallas_call(
        _ssd_state_sample_body,
        grid=(r,),
        in_specs=[vec(SSD_CONV_DIM), vec(LANES), const(LANES), const(D_INNER), st_spec],
        out_specs=[vec(D_INNER), st_spec],
        out_shape=[jax.ShapeDtypeStruct((r, 1, D_INNER), F32), jax.ShapeDtypeStruct(st_shape, F32)],
        compiler_params=_params("arbitrary"),
        name="ssd_state_sample",
    )(xbc.reshape(r, 1, SSD_CONV_DIM), dt.reshape(r, 1, LANES), a_log, d_skip,
      state.reshape(st_shape))
    (x_new,) = _call_whole(_ssd_out_sample_body, "ssd_out_sample",
                           [jax.ShapeDtypeStruct((r, D_MODEL), F32)],
                           x, y.reshape(r, D_INNER), z, norm_w, w_out)
    return x_new, new_conv, new_state.reshape(r, SSD_HEADS, SSD_HEAD_DIM, SSD_STATE)


def _qkv_sample_body(x_ref, nw_ref, w_ref, cos_ref, sdn_ref, sup_ref, q_ref, k_ref, v_ref):
    h = _rms(x_ref[...], nw_ref[...]).astype(BF16)
    rows = x_ref.shape[0]
    tab = lambda r: jnp.broadcast_to(r[...], (rows, LANES))
    cos, sdn, sup = tab(cos_ref), tab(sdn_ref), tab(sup_ref)
    q = jnp.dot(h, w_ref[:, :D_MODEL], preferred_element_type=F32)
    q_ref[...] = _rope(q, cos, sdn, sup)
    k = jnp.dot(h, w_ref[:, D_MODEL:2 * D_MODEL], preferred_element_type=F32)
    k_ref[...] = _rope(k, cos, sdn, sup)
    v_ref[...] = jnp.dot(h, w_ref[:, 2 * D_MODEL:], preferred_element_type=F32)


N_PAST_BLK = PAST_LEN // MOBA_BLOCK
PAGES_PER_BLK = MOBA_BLOCK // PAGE_SIZE


def _moba_sample_body(pt_ref, q_ref, kn_ref, vn_ref, k0_ref, k1_ref, v0_ref, v1_ref, o_ref,
                      s_ref, gate_ref, p_ref, acc_ref, pown_ref):
    del pt_ref
    n = pl.program_id(1)
    blk = MOBA_BLOCK
    q = q_ref[0]
    page3 = (PAGE_SIZE, ATTN_HEADS, LANES)
    on_diag = (lax.broadcasted_iota(jnp.int32, page3, 0)
               == lax.broadcasted_iota(jnp.int32, page3, 2))

    @pl.when(n < N_PAST_BLK)
    def _():
        @pl.when(n == 0)
        def _():
            gate_ref[...] = jnp.zeros_like(gate_ref)

        k_sum = jnp.zeros((ATTN_HEADS, ATTN_HEAD_DIM), F32)
        for w, k_ref in enumerate((k0_ref, k1_ref)):
            k = k_ref[0]
            s3 = jnp.sum(k * q[None], axis=-1, keepdims=True) * ATTN_SCALE
            dense = jnp.sum(jnp.where(on_diag, s3, 0.0), axis=0)
            t0 = pl.multiple_of(n * blk + w * PAGE_SIZE, PAGE_SIZE)
            s_ref[:, pl.ds(t0, PAGE_SIZE)] = dense
            k_sum = k_sum + jnp.sum(k, axis=0)
        g_n = jnp.sum(q * (k_sum / blk), axis=-1, keepdims=True)
        lane = lax.broadcasted_iota(jnp.int32, gate_ref.shape, 1)
        gate_ref[...] = jnp.where(lane == n, g_n, gate_ref[...])

    @pl.when(n == N_PAST_BLK)
    def _():
        gate = gate_ref[...]
        s_own = jnp.sum(q * kn_ref[0], axis=-1, keepdims=True) * ATTN_SCALE
        m = s_own
        masked = []
        for b in range(N_PAST_BLK):
            sel = _moba_select(gate, b, N_PAST_BLK)
            sb = jnp.where(sel, s_ref[:, b * blk:(b + 1) * blk], NEG_INF)
            masked.append(sb)
            m = jnp.maximum(m, jnp.max(sb, axis=1, keepdims=True))
        p_own = jnp.exp(s_own - m)
        l = p_own
        ps = []
        for sb in masked:
            pb = jnp.exp(sb - m)
            ps.append(pb)
            l = l + jnp.sum(pb, axis=1, keepdims=True)
        for b in range(N_PAST_BLK):
            p_ref[:, b * blk:(b + 1) * blk] = ps[b] / l
        pown_ref[...] = jnp.broadcast_to(p_own / l, pown_ref.shape)
        acc_ref[...] = jnp.zeros_like(acc_ref)

    @pl.when(n >= N_PAST_BLK)
    def _():
        acc = acc_ref[...]
        for w, v_ref in enumerate((v0_ref, v1_ref)):
            t0 = pl.multiple_of((n - N_PAST_BLK) * blk + w * PAGE_SIZE, PAGE_SIZE)
            p_dense = p_ref[:, pl.ds(t0, PAGE_SIZE)]
            p3 = jnp.sum(jnp.where(on_diag, p_dense[None], 0.0), axis=-1, keepdims=True)
            acc = acc + jnp.sum(p3 * v_ref[0], axis=0)
        acc_ref[...] = acc

    @pl.when(n == 2 * N_PAST_BLK - 1)
    def _():
        o_ref[0] = acc_ref[...] + pown_ref[:, 0:1] * vn_ref[0]


def _moba_sample(q, k_new, v_new, cache_k, cache_v, page_table):
    r = DEC_BATCH
    heads = (ATTN_HEADS, ATTN_HEAD_DIM)
    vec = pl.BlockSpec((1,) + heads, lambda b, n, pt: (b, 0, 0))

    def page(which, is_v):
        def index(b, n, pt):
            blk = jnp.clip(n - N_PAST_BLK, 0, N_PAST_BLK - 1) if is_v else jnp.minimum(n, N_PAST_BLK - 1)
            return (pt[b * N_PAGES + blk * PAGES_PER_BLK + which], 0, 0, 0)
        return pl.BlockSpec((1, PAGE_SIZE) + heads, index)

    grid_spec = pltpu.PrefetchScalarGridSpec(
        num_scalar_prefetch=1,
        grid=(r, 2 * N_PAST_BLK),
        in_specs=[vec, vec, vec, page(0, False), page(1, False), page(0, True), page(1, True)],
        out_specs=vec,
        scratch_shapes=[
            pltpu.VMEM((ATTN_HEADS, PAST_LEN), F32),
            pltpu.VMEM((ATTN_HEADS, LANES), F32),
            pltpu.VMEM((ATTN_HEADS, PAST_LEN), F32),
            pltpu.VMEM(heads, F32),
            pltpu.VMEM((ATTN_HEADS, LANES), F32),
        ],
    )
    to3 = lambda a: a.reshape((r,) + heads)
    out = pl.pallas_call(
        _moba_sample_body,
        grid_spec=grid_spec,
        out_shape=jax.ShapeDtypeStruct((r,) + heads, F32),
        compiler_params=_params("arbitrary", "arbitrary"),
        name="moba_sample",
    )(page_table.reshape(-1), to3(q), to3(k_new), to3(v_new), cache_k, cache_k, cache_v, cache_v)
    return out.reshape(r, D_MODEL)


def _rope_tables(pos):
    half = ROT_DIM // 2
    inv_freq = ROPE_THETA ** (-(jnp.arange(half, dtype=F32) * 2.0) / ROT_DIM)
    ang = pos.astype(F32)[:, None] * inv_freq
    cos, sin = jnp.cos(ang), jnp.sin(ang)
    ones = jnp.ones((pos.shape[0], ATTN_HEAD_DIM - ROT_DIM), F32)
    zeros = jnp.zeros((pos.shape[0], ATTN_HEAD_DIM - ROT_DIM), F32)
    zh = jnp.zeros_like(sin)
    reps = LANES // ATTN_HEAD_DIM
    tile = lambda a: jnp.tile(a, (1, reps))
    return (tile(jnp.concatenate([cos, cos, ones], axis=1)),
            tile(jnp.concatenate([-sin, zh, zeros], axis=1)),
            tile(jnp.concatenate([zh, sin, zeros], axis=1)))


def _pad_lanes(a):
    return jnp.pad(a, [(0, 0)] * (a.ndim - 1) + [(0, LANES - a.shape[-1])])


def kernel(x_prompt, x_sample, state_sconv, state_ssm, state_ssm_conv, cache_k, cache_v, state_ffn_conv, page_table, norm_mix_w, norm_ffn_w, norm_final_w, sconv_w_in, sconv_conv_w, sconv_w_out, ssd_w_in, ssd_conv_w, ssd_conv_b, ssd_dt_bias, ssd_a_log, ssd_d, ssd_norm_w, ssd_w_out, attn_w_qkv, attn_w_o, ffn_w_up, ffn_conv_w, ffn_conv_b, ffn_w_down):
    xp = x_prompt.reshape(N_TOK, D_MODEL)
    xs = x_sample.reshape(DEC_BATCH, D_MODEL)
    fnw = norm_final_w.reshape(1, D_MODEL)
    sconv_p, sconv_s, ffnc_p, ffnc_s = [], [], [], []
    for i in range(DEPTH):
        kind, j = i % N_MIXERS, i // N_MIXERS
        nw = norm_mix_w[i].reshape(1, D_MODEL)
        if kind == 0:
            w_in, w_out = sconv_w_in[j].astype(BF16), sconv_w_out[j].astype(BF16)
            xp, st = _sconv_prompt(xp, nw, w_in, sconv_conv_w[j], w_out)
            xs, st_s = _sconv_sample(xs, nw, w_in, sconv_conv_w[j], w_out, state_sconv[j])
            sconv_p.append(st)
            sconv_s.append(st_s)
        elif kind == 1:
            w_in = ssd_w_in[j]
            w_z = w_in[:, :D_INNER].astype(BF16)
            w_x = w_in[:, D_INNER:D_INNER + SSD_CONV_DIM].astype(BF16)
            w_dt = _pad_lanes(w_in[:, D_INNER + SSD_CONV_DIM:]).astype(BF16)
            w_out = ssd_w_out[j].astype(BF16)
            conv_b = ssd_conv_b[j].reshape(1, SSD_CONV_DIM)
            dt_bias = _pad_lanes(ssd_dt_bias[j].reshape(1, SSD_HEADS))
            a_log = _pad_lanes(ssd_a_log[j].reshape(1, SSD_HEADS))
            d_skip = jnp.repeat(ssd_d[j], SSD_HEAD_DIM).reshape(1, D_INNER)
            norm_w = ssd_norm_w[j].reshape(1, D_INNER)
            z, xbc, dt, ssmc_p = _ssd_inproj_prompt(xp, nw, w_z, w_x, w_dt, ssd_conv_w[j], conv_b, dt_bias)
            xp, ssm_p = _ssd_scan_prompt(xp, z, xbc, dt, a_log, d_skip, norm_w, w_out)
            ssm_p = ssm_p.reshape(BATCH, SSD_HEADS, SSD_HEAD_DIM, SSD_STATE)
            xs, ssmc_s, ssm_s = _ssd_sample(xs, nw, w_z, w_x, w_dt, ssd_conv_w[j], conv_b, dt_bias,
                                            a_log, d_skip, norm_w, w_out, state_ssm[j], state_ssm_conv[j])
        else:
            w_qkv, w_o = attn_w_qkv[j].astype(BF16), attn_w_o[j].astype(BF16)
            q, k_p, v_p = _qkv_prompt(xp, nw, w_qkv, *_rope_tables(jnp.arange(SEQ)))
            xp = _proj_residual(_moba_prompt(q, k_p, v_p), w_o, xp, PROJ_TL)
            sds = jax.ShapeDtypeStruct((DEC_BATCH, D_MODEL), F32)
            q_s, k_s, v_s = _call_whole(_qkv_sample_body, "qkv_sample", [sds, sds, sds],
                                        xs, nw, w_qkv, *_rope_tables(jnp.full((1,), PAST_LEN)))
            o_s = _moba_sample(q_s, k_s, v_s, cache_k[j], cache_v[j], page_table)
            xs = _proj_residual(o_s, w_o, xs, DEC_BATCH)
        nwf = norm_ffn_w[i].reshape(1, D_MODEL)
        w_up, w_down = ffn_w_up[i].astype(BF16), ffn_w_down[i].astype(BF16)
        conv_b = ffn_conv_b[i].reshape(1, 2 * D_FF)
        final = i == DEPTH - 1
        xp, fc_p = _ffn_prompt(xp, nwf, w_up, ffn_conv_w[i], conv_b, w_down, fnw, final)
        xs, fc_s = _ffn_sample(xs, nwf, w_up, ffn_conv_w[i], conv_b, w_down, fnw, state_ffn_conv[i], final)
        ffnc_p.append(fc_p)
        ffnc_s.append(fc_s)
    kv_p = (1, BATCH, SEQ, ATTN_HEADS, ATTN_HEAD_DIM)
    kv_s = (1, DEC_BATCH, 1, ATTN_HEADS, ATTN_HEAD_DIM)
    return (xp.reshape(BATCH, SEQ, D_MODEL), xs.reshape(DEC_BATCH, 1, D_MODEL),
            jnp.stack(sconv_p), jnp.stack(sconv_s),
            ssm_p[None], ssm_s[None], ssmc_p[None], ssmc_s[None],
            k_p.reshape(kv_p), v_p.reshape(kv_p), k_s.reshape(kv_s), v_s.reshape(kv_s),
            jnp.stack(ffnc_p), jnp.stack(ffnc_s))
```

```python
import functools

import jax
import jax.numpy as jnp
from jax import lax
from jax.experimental import pallas as pl
from jax.experimental.pallas import tpu as pltpu

F32 = jnp.float32
BF16 = jnp.bfloat16

D_MODEL = 1024
BATCH = 8
SEQ = 2048
DEPTH = 4
DEC_BATCH = 128
PAST_LEN = 2048
PAGE_SIZE = 128
N_PAGES = PAST_LEN // PAGE_SIZE
N_MIXERS = 3
RMS_EPS = 1e-6
SCONV_WIDTH = 3
D_INNER = 2 * D_MODEL
SSD_HEAD_DIM = 64
SSD_HEADS = D_INNER // SSD_HEAD_DIM
SSD_GROUPS = 4
SSD_STATE = 128
SSD_CONV_WIDTH = 4
SSD_CONV_DIM = D_INNER + 2 * SSD_GROUPS * SSD_STATE
SSD_CHUNK = 128
SSD_PAIRS = SSD_HEADS // 2
ATTN_HEAD_DIM = 64
ATTN_HEADS = D_MODEL // ATTN_HEAD_DIM
MOBA_BLOCK = 256
MOBA_TOP_K = 3
ROPE_THETA = 500000.0
ROT_DIM = ATTN_HEAD_DIM // 4
D_FF = 2816
FFN_CONV_WIDTH = 3

LANES = 128
SUBLANES = 8
VMEM_LIMIT = 56 * 1024 * 1024
N_TOK = BATCH * SEQ
NEG_INF = float("-inf")


def _params(*sem):
    return pltpu.CompilerParams(dimension_semantics=sem, vmem_limit_bytes=VMEM_LIMIT)


def _bdot(a, b):
    return jnp.dot(a.astype(BF16), b.astype(BF16), preferred_element_type=F32)


def _bdot_nt(a, b):
    return lax.dot_general(a.astype(BF16), b.astype(BF16), (((1,), (1,)), ((), ())),
                           preferred_element_type=F32)


def _rms(x, w):
    inv = lax.rsqrt(jnp.mean(x * x, axis=-1, keepdims=True) + RMS_EPS)
    return (x * inv) * w


def _silu(x):
    return x * (1.0 / (1.0 + jnp.exp(-x)))


def _softplus(x):
    return jnp.maximum(x, 0.0) + jnp.log1p(jnp.exp(-jnp.abs(x)))


def _conv_rows(u, w, prev):
    width = w.shape[0]
    row8 = lax.broadcasted_iota(jnp.int32, (SUBLANES, u.shape[1]), 0)
    out = u * w[width - 1:width, :]
    for k in range(1, width):
        sh = pltpu.roll(u, k, axis=0)
        head = sh[:SUBLANES]
        for t in range(k):
            src = SUBLANES - k + t
            head = jnp.where(row8 == t, prev[src:src + 1, :], head)
        sh = jnp.concatenate([head, sh[SUBLANES:]], axis=0)
        out = out + sh * w[width - 1 - k:width - k, :]
    return out


def _conv_step(u, w, past):
    width = w.shape[0]
    out = u * w[width - 1:width, :]
    for k in range(width - 1):
        out = out + past[k] * w[k:k + 1, :]
    return out


SCONV_TL = 512


def _sconv_prompt_body(x_ref, nw_ref, win_ref, cw_ref, wout_ref, o_ref, st_ref, carry_ref):
    l = pl.program_id(1)

    @pl.when(l == 0)
    def _():
        carry_ref[...] = jnp.zeros_like(carry_ref)

    x = x_ref[...]
    h = _rms(x, nw_ref[...]).astype(BF16)
    p = jnp.dot(h, win_ref[...], preferred_element_type=F32)
    g = p[:, D_MODEL:2 * D_MODEL] * p[:, 2 * D_MODEL:]
    c = _conv_rows(g, cw_ref[...], carry_ref[...])
    carry_ref[...] = g[SCONV_TL - SUBLANES:, :]
    o_ref[...] = x + _bdot(p[:, :D_MODEL] * c, wout_ref[...])

    @pl.when(l == pl.num_programs(1) - 1)
    def _():
        st_ref[0] = g[SCONV_TL - (SCONV_WIDTH - 1):, :]


def _sconv_prompt(x, nw, w_in, conv_w, w_out):
    nl = SEQ // SCONV_TL
    row = lambda b, l: (b * nl + l, 0)
    const = lambda b, l: (0, 0)
    return pl.pallas_call(
        _sconv_prompt_body,
        grid=(BATCH, nl),
        in_specs=[
            pl.BlockSpec((SCONV_TL, D_MODEL), row),
            pl.BlockSpec((1, D_MODEL), const),
            pl.BlockSpec((D_MODEL, 3 * D_MODEL), const),
            pl.BlockSpec((SCONV_WIDTH, D_MODEL), const),
            pl.BlockSpec((D_MODEL, D_MODEL), const),
        ],
        out_specs=[
            pl.BlockSpec((SCONV_TL, D_MODEL), row),
            pl.BlockSpec((1, SCONV_WIDTH - 1, D_MODEL), lambda b, l: (b, 0, 0)),
        ],
        out_shape=[
            jax.ShapeDtypeStruct((N_TOK, D_MODEL), F32),
            jax.ShapeDtypeStruct((BATCH, SCONV_WIDTH - 1, D_MODEL), F32),
        ],
        scratch_shapes=[pltpu.VMEM((SUBLANES, D_MODEL), F32)],
        compiler_params=_params("arbitrary", "arbitrary"),
        name="sconv_prompt",
    )(x, nw, w_in, conv_w, w_out)


FFN_TL = 512
FFN_CHUNK = D_FF // 2


def _ffn_prompt_body(x_ref, nw_ref, wup_ref, cw_ref, cb_ref, wdn_ref, fnw_ref, o_ref, st_ref,
                     carry_ref, *, final):
    l = pl.program_id(1)

    @pl.when(l == 0)
    def _():
        carry_ref[...] = jnp.zeros_like(carry_ref)

    x = x_ref[...]
    h = _rms(x, nw_ref[...]).astype(BF16)
    acc = x
    last = l == pl.num_programs(1) - 1
    for c in range(D_FF // FFN_CHUNK):
        halves = []
        for base in (0, D_FF):
            lo = base + c * FFN_CHUNK
            hi = lo + FFN_CHUNK
            u = jnp.dot(h, wup_ref[:, lo:hi], preferred_element_type=F32)
            conv = _conv_rows(u, cw_ref[:, lo:hi], carry_ref[:, lo:hi]) + cb_ref[:, lo:hi]
            carry_ref[:, lo:hi] = u[FFN_TL - SUBLANES:, :]

            @pl.when(last)
            def _(u=u, lo=lo, hi=hi):
                st_ref[0, :, lo:hi] = u[FFN_TL - (FFN_CONV_WIDTH - 1):, :]

            halves.append(conv)
        a, g = halves
        acc = acc + _bdot(_silu(g) * a, wdn_ref[c * FFN_CHUNK:(c + 1) * FFN_CHUNK, :])
    if final:
        acc = _rms(acc, fnw_ref[...])
    o_ref[...] = acc


def _ffn_prompt(x, nw, w_up, conv_w, conv_b, w_down, fnw, final):
    nl = SEQ // FFN_TL
    row = lambda b, l: (b * nl + l, 0)
    const = lambda b, l: (0, 0)
    once = pl.Buffered(1)
    return pl.pallas_call(
        functools.partial(_ffn_prompt_body, final=final),
        grid=(BATCH, nl),
        in_specs=[
            pl.BlockSpec((FFN_TL, D_MODEL), row),
            pl.BlockSpec((1, D_MODEL), const),
            pl.BlockSpec((D_MODEL, 2 * D_FF), const, pipeline_mode=once),
            pl.BlockSpec((FFN_CONV_WIDTH, 2 * D_FF), const),
            pl.BlockSpec((1, 2 * D_FF), const),
            pl.BlockSpec((D_FF, D_MODEL), const, pipeline_mode=once),
            pl.BlockSpec((1, D_MODEL), const),
        ],
        out_specs=[
            pl.BlockSpec((FFN_TL, D_MODEL), row),
            pl.BlockSpec((1, FFN_CONV_WIDTH - 1, 2 * D_FF), lambda b, l: (b, 0, 0)),
        ],
        out_shape=[
            jax.ShapeDtypeStruct((N_TOK, D_MODEL), F32),
            jax.ShapeDtypeStruct((BATCH, FFN_CONV_WIDTH - 1, 2 * D_FF), F32),
        ],
        scratch_shapes=[pltpu.VMEM((SUBLANES, 2 * D_FF), F32)],
        compiler_params=_params("arbitrary", "arbitrary"),
        name="ffn_prompt",
    )(x, nw, w_up, conv_w, conv_b, w_down, fnw)


SSD_TL = 256


def _ssd_inproj_prompt_body(x_ref, nw_ref, wz_ref, wx_ref, wdt_ref, cw_ref, cb_ref, dtb_ref,
                            z_ref, xbc_ref, dt_ref, st_ref, carry_ref):
    l = pl.program_id(1)

    @pl.when(l == 0)
    def _():
        carry_ref[...] = jnp.zeros_like(carry_ref)

    h = _rms(x_ref[...], nw_ref[...]).astype(BF16)
    z_ref[...] = jnp.dot(h, wz_ref[...], preferred_element_type=F32)
    dt_ref[...] = _softplus(jnp.dot(h, wdt_ref[...], preferred_element_type=F32) + dtb_ref[...])
    u = jnp.dot(h, wx_ref[...], preferred_element_type=F32)
    xbc_ref[...] = _silu(_conv_rows(u, cw_ref[...], carry_ref[...]) + cb_ref[...])
    carry_ref[...] = u[SSD_TL - SUBLANES:, :]

    @pl.when(l == pl.num_programs(1) - 1)
    def _():
        st_ref[0] = u[SSD_TL - (SSD_CONV_WIDTH - 1):, :]


def _ssd_inproj_prompt(x, nw, w_z, w_x, w_dt, conv_w, conv_b, dt_bias):
    nl = SEQ // SSD_TL
    row = lambda b, l: (b * nl + l, 0)
    const = lambda b, l: (0, 0)
    return pl.pallas_call(
        _ssd_inproj_prompt_body,
        grid=(BATCH, nl),
        in_specs=[
            pl.BlockSpec((SSD_TL, D_MODEL), row),
            pl.BlockSpec((1, D_MODEL), const),
            pl.BlockSpec((D_MODEL, D_INNER), const),
            pl.BlockSpec((D_MODEL, SSD_CONV_DIM), const),
            pl.BlockSpec((D_MODEL, LANES), const),
            pl.BlockSpec((SSD_CONV_WIDTH, SSD_CONV_DIM), const),
            pl.BlockSpec((1, SSD_CONV_DIM), const),
            pl.BlockSpec((1, LANES), const),
        ],
        out_specs=[
            pl.BlockSpec((SSD_TL, D_INNER), row),
            pl.BlockSpec((SSD_TL, SSD_CONV_DIM), row),
            pl.BlockSpec((SSD_TL, LANES), row),
            pl.BlockSpec((1, SSD_CONV_WIDTH - 1, SSD_CONV_DIM), lambda b, l: (b, 0, 0)),
        ],
        out_shape=[
            jax.ShapeDtypeStruct((N_TOK, D_INNER), F32),
            jax.ShapeDtypeStruct((N_TOK, SSD_CONV_DIM), F32),
            jax.ShapeDtypeStruct((N_TOK, LANES), F32),
            jax.ShapeDtypeStruct((BATCH, SSD_CONV_WIDTH - 1, SSD_CONV_DIM), F32),
        ],
        scratch_shapes=[pltpu.VMEM((SUBLANES, SSD_CONV_DIM), F32)],
        compiler_params=_params("arbitrary", "arbitrary"),
        name="ssd_inproj_prompt",
    )(x, nw, w_z, w_x, w_dt, conv_w, conv_b, dt_bias)


def _cumsum_rows(x):
    n = x.shape[0]
    row = lax.broadcasted_iota(jnp.int32, x.shape, 0)
    k = 1
    while k < n:
        x = x + jnp.where(row >= k, pltpu.roll(x, k, axis=0), 0.0)
        k *= 2
    return x


def _pair_lanes(vals, h0, shape):
    lane = lax.broadcasted_iota(jnp.int32, shape, 1)
    return jnp.where(lane < SSD_HEAD_DIM, vals[:, h0:h0 + 1], vals[:, h0 + 1:h0 + 2])


def _group_norm_gate(y, z, nw):
    y = y * _silu(z)
    gw = D_INNER // SSD_GROUPS
    outs = []
    for g in range(SSD_GROUPS):
        yg = y[:, g * gw:(g + 1) * gw]
        outs.append(_rms(yg, nw[:, g * gw:(g + 1) * gw]))
    return jnp.concatenate(outs, axis=1)


def _ssd_scan_prompt_body(x_ref, z_ref, xbc_ref, dt_ref, alog_ref, dskip_ref, nw_ref, wout_ref,
                          o_ref, st_ref, y_ref):
    c = pl.program_id(1)

    @pl.when(c == 0)
    def _():
        st_ref[...] = jnp.zeros_like(st_ref)

    q = SSD_CHUNK
    dt = dt_ref[...]
    acum = _cumsum_rows(dt * (-jnp.exp(alog_ref[...])))
    acum_t = acum.T
    e_acum = jnp.exp(acum)
    last = acum[q - 1:q, :]
    e_end = jnp.exp(last - acum)
    e_chunk = jnp.exp(last)
    row = lax.broadcasted_iota(jnp.int32, (q, q), 0)
    col = lax.broadcasted_iota(jnp.int32, (q, q), 1)
    causal = row >= col
    lane_lo = col < SSD_HEAD_DIM
    row_lo = row < SSD_HEAD_DIM

    for g in range(SSD_GROUPS):
        b_g = xbc_ref[:, D_INNER + g * SSD_STATE:D_INNER + (g + 1) * SSD_STATE].astype(BF16)
        c_g = xbc_ref[:, D_INNER + (SSD_GROUPS + g) * SSD_STATE:
                      D_INNER + (SSD_GROUPS + g + 1) * SSD_STATE].astype(BF16)
        cb = _bdot_nt(c_g, b_g)
        for j in range(g * SSD_PAIRS // SSD_GROUPS, (g + 1) * SSD_PAIRS // SSD_GROUPS):
            h0 = 2 * j
            xs = xbc_ref[:, j * LANES:(j + 1) * LANES]
            xdt = xs * _pair_lanes(dt, h0, (q, LANES))
            xdt_b = xdt.astype(BF16)
            yd = []
            for h in (h0, h0 + 1):
                seg = acum[:, h:h + 1] - acum_t[h:h + 1, :]
                decay = jnp.exp(jnp.where(causal, seg, NEG_INF))
                yd.append(_bdot(cb * decay, xdt_b))
            y = jnp.where(lane_lo, yd[0], yd[1])
            state = st_ref[0, j]
            y = y + _bdot_nt(c_g, state) * _pair_lanes(e_acum, h0, (q, LANES))
            y = y + xs * dskip_ref[:, j * LANES:(j + 1) * LANES]
            y_ref[:, j * LANES:(j + 1) * LANES] = y
            xw = xdt * _pair_lanes(e_end, h0, (q, LANES))
            upd = _bdot(xw.T, b_g)
            scale = jnp.where(row_lo, e_chunk[:, h0:h0 + 1], e_chunk[:, h0 + 1:h0 + 2])
            st_ref[0, j] = state * scale + upd

    yn = _group_norm_gate(y_ref[...], z_ref[...], nw_ref[...])
    o_ref[...] = x_ref[...] + _bdot(yn, wout_ref[...])


def _ssd_scan_prompt(x, z, xbc, dt, a_log, d_skip, nw, w_out):
    nc = SEQ // SSD_CHUNK
    row = lambda b, c: (b * nc + c, 0)
    const = lambda b, c: (0, 0)
    return pl.pallas_call(
        _ssd_scan_prompt_body,
        grid=(BATCH, nc),
        in_specs=[
            pl.BlockSpec((SSD_CHUNK, D_MODEL), row),
            pl.BlockSpec((SSD_CHUNK, D_INNER), row),
            pl.BlockSpec((SSD_CHUNK, SSD_CONV_DIM), row),
            pl.BlockSpec((SSD_CHUNK, LANES), row),
            pl.BlockSpec((1, LANES), const),
            pl.BlockSpec((1, D_INNER), const),
            pl.BlockSpec((1, D_INNER), const),
            pl.BlockSpec((D_INNER, D_MODEL), const),
        ],
        out_specs=[
            pl.BlockSpec((SSD_CHUNK, D_MODEL), row),
            pl.BlockSpec((1, SSD_PAIRS, 2 * SSD_HEAD_DIM, SSD_STATE), lambda b, c: (b, 0, 0, 0)),
        ],
        out_shape=[
            jax.ShapeDtypeStruct((N_TOK, D_MODEL), F32),
            jax.ShapeDtypeStruct((BATCH, SSD_PAIRS, 2 * SSD_HEAD_DIM, SSD_STATE), F32),
        ],
        scratch_shapes=[pltpu.VMEM((SSD_CHUNK, D_INNER), F32)],
        compiler_params=_params("arbitrary", "arbitrary"),
        name="ssd_scan_prompt",
    )(x, z, xbc, dt, a_log, d_skip, nw, w_out)


QKV_TL = 512


def _rope_t(t, cos, sin_dn, sin_up):
    reps = t.shape[0] // ATTN_HEAD_DIM
    half = ROT_DIM // 2
    tile = lambda a: jnp.concatenate([a] * reps, axis=0)
    return (t * tile(cos)
            + pltpu.roll(t, t.shape[0] - half, axis=0) * tile(sin_dn)
            + pltpu.roll(t, half, axis=0) * tile(sin_up))


def _qkv_t(x, nw, wt_ref, cos, sdn, sup):
    ht = _rms(x, nw).T.astype(BF16)
    qt = _rope_t(jnp.dot(wt_ref[:D_MODEL, :], ht, preferred_element_type=F32), cos, sdn, sup)
    kt = _rope_t(jnp.dot(wt_ref[D_MODEL:2 * D_MODEL, :], ht, preferred_element_type=F32), cos, sdn, sup)
    vt = jnp.dot(wt_ref[2 * D_MODEL:, :], ht, preferred_element_type=F32)
    return qt, kt, vt


def _qkv_prompt_body(x_ref, nw_ref, wt_ref, cos_ref, sdn_ref, sup_ref, qt_ref, kt_ref, vt_ref, k_ref):
    qt, kt, vt = _qkv_t(x_ref[...], nw_ref[...], wt_ref, cos_ref[...], sdn_ref[...], sup_ref[...])
    qt_ref[0] = qt
    kt_ref[0] = kt
    vt_ref[0] = vt
    k_ref[...] = kt.T


def _qkv_prompt(x, nw, w_qkv_t, cos, sdn, sup):
    nl = SEQ // QKV_TL
    row = lambda b, l: (b * nl + l, 0)
    const = lambda b, l: (0, 0)
    tab = pl.BlockSpec((ATTN_HEAD_DIM, QKV_TL), lambda b, l: (0, l))
    tspec = pl.BlockSpec((1, D_MODEL, QKV_TL), lambda b, l: (b, 0, l))
    tout = jax.ShapeDtypeStruct((BATCH, D_MODEL, SEQ), F32)
    return pl.pallas_call(
        _qkv_prompt_body,
        grid=(BATCH, nl),
        in_specs=[
            pl.BlockSpec((QKV_TL, D_MODEL), row),
            pl.BlockSpec((1, D_MODEL), const),
            pl.BlockSpec((3 * D_MODEL, D_MODEL), const),
            tab, tab, tab,
        ],
        out_specs=[tspec, tspec, tspec, pl.BlockSpec((QKV_TL, D_MODEL), row)],
        out_shape=[tout, tout, tout, jax.ShapeDtypeStruct((N_TOK, D_MODEL), F32)],
        compiler_params=_params("arbitrary", "arbitrary"),
        name="qkv_prompt",
    )(x, nw, w_qkv_t, cos, sdn, sup)


N_BLK = SEQ // MOBA_BLOCK
ATTN_SCALE = ATTN_HEAD_DIM ** -0.5


def _moba_select(gate, n, own):
    ax = gate.ndim - 1
    lane = lax.broadcasted_iota(jnp.int32, gate.shape, ax)
    g_n = jnp.sum(jnp.where(lane == n, gate, 0.0), axis=ax, keepdims=True)
    ahead = (gate > g_n) | ((gate == g_n) & (lane < n))
    rank = jnp.sum(jnp.where(ahead & (lane < own), 1.0, 0.0), axis=ax, keepdims=True)
    return rank < MOBA_TOP_K


def _moba_bias_rows(gate_t, own):
    row = lax.broadcasted_iota(jnp.int32, gate_t.shape, 0)
    past = row < own
    bias = jnp.full(gate_t.shape, NEG_INF, F32)
    for n in range(gate_t.shape[0] - 1):
        g_n = gate_t[n:n + 1, :]
        ahead = (gate_t > g_n) | ((gate_t == g_n) & (row < n))
        rank = jnp.sum(jnp.where(ahead & past, 1.0, 0.0), axis=0, keepdims=True)
        chosen = (rank < MOBA_TOP_K) & (row == n) & past
        bias = jnp.where(chosen, 0.0, bias)
    return bias


def _dot3(a, b):
    a_hi = a.astype(BF16)
    b_hi = b.astype(BF16)
    a_lo = (a - a_hi.astype(F32)).astype(BF16)
    b_lo = (b - b_hi.astype(F32)).astype(BF16)
    return (jnp.dot(a_hi, b_hi, preferred_element_type=F32)
            + jnp.dot(a_hi, b_lo, preferred_element_type=F32)
            + jnp.dot(a_lo, b_hi, preferred_element_type=F32))


def _moba_prompt_body(qt_ref, k_ref, vt_ref, ot_ref, kmean_ref, bias_ref):
    i = pl.program_id(2)
    blk = MOBA_BLOCK
    dh = ATTN_HEAD_DIM

    @pl.when(i == 0)
    def _():
        for n in range(N_BLK):
            kmean_ref[n:n + 1, :] = jnp.mean(k_ref[n * blk:(n + 1) * blk, :], axis=0, keepdims=True)

    qt = qt_ref[0]
    head_row = lax.broadcasted_iota(jnp.int32, qt.shape, 0) < dh
    km = kmean_ref[...]
    head_lane = lax.broadcasted_iota(jnp.int32, km.shape, 1) < dh
    qs = qt * ATTN_SCALE
    q_w = [jnp.where(head_row, qs, 0.0).astype(BF16), jnp.where(head_row, 0.0, qs).astype(BF16)]
    for hh in range(2):
        km_h = jnp.where(head_lane, km, 0.0) if hh == 0 else jnp.where(head_lane, 0.0, km)
        bias_ref[hh] = _moba_bias_rows(_dot3(km_h, qt), i)

    key = lax.broadcasted_iota(jnp.int32, (blk, blk), 0)
    qry = lax.broadcasted_iota(jnp.int32, (blk, blk), 1)

    def scores(n):
        k_n = k_ref[pl.ds(pl.multiple_of(n * blk, blk), blk), :].astype(BF16)
        return [jnp.dot(k_n, q_w[hh], preferred_element_type=F32) for hh in range(2)]

    def absorb(n, masked, state):
        vt_n = vt_ref[0, :, pl.ds(pl.multiple_of(n * blk, blk), blk)].astype(BF16)
        out = []
        for hh in range(2):
            m, l, acc = state[3 * hh:3 * hh + 3]
            s = masked[hh]
            m_new = jnp.maximum(m, jnp.max(s, axis=0, keepdims=True))
            m_ref = jnp.where(m_new == NEG_INF, 0.0, m_new)
            alpha = jnp.exp(m - m_ref)
            p = jnp.exp(s - m_ref)
            l = l * alpha + jnp.sum(p, axis=0, keepdims=True)
            acc = acc * alpha + jnp.dot(vt_n[hh * dh:(hh + 1) * dh, :], p.astype(BF16),
                                        preferred_element_type=F32)
            out += [m_new, l, acc]
        return out

    def past_block(n, carry):
        nxt = scores(n + 1)
        masked = [carry[6 + hh] + bias_ref[hh, pl.ds(n, 1), :] for hh in range(2)]
        return tuple(absorb(n, masked, carry[:6]) + nxt)

    row0 = jnp.full((1, blk), NEG_INF, F32)
    zrow = jnp.zeros((1, blk), F32)
    zacc = jnp.zeros((dh, blk), F32)
    carry = lax.fori_loop(0, i, past_block, (row0, zrow, zacc, row0, zrow, zacc) + tuple(scores(0)))
    own = [jnp.where(key <= qry, carry[6 + hh], NEG_INF) for hh in range(2)]
    m0, l0, acc0, m1, l1, acc1 = absorb(i, own, carry[:6])
    ot_ref[0] = jnp.concatenate([acc0 / l0, acc1 / l1], axis=0)


def _moba_prompt(qt, k, vt):
    pairs = ATTN_HEADS // 2
    qspec = pl.BlockSpec((1, LANES, MOBA_BLOCK), lambda b, p, i: (b, p, i))
    return pl.pallas_call(
        _moba_prompt_body,
        grid=(BATCH, pairs, N_BLK),
        in_specs=[
            qspec,
            pl.BlockSpec((SEQ, LANES), lambda b, p, i: (b, p)),
            pl.BlockSpec((1, LANES, SEQ), lambda b, p, i: (b, p, 0)),
        ],
        out_specs=qspec,
        out_shape=jax.ShapeDtypeStruct((BATCH, D_MODEL, SEQ), F32),
        scratch_shapes=[pltpu.VMEM((N_BLK, LANES), F32), pltpu.VMEM((2, N_BLK, MOBA_BLOCK), F32)],
        compiler_params=_params("arbitrary", "arbitrary", "arbitrary"),
        name="moba_prompt",
    )(qt, k, vt)


PROJ_TL = 512


def _proj_residual_t_body(at_ref, w_ref, x_ref, o_ref):
    o_ref[...] = x_ref[...] + _bdot(at_ref[0].T, w_ref[...])


def _proj_residual_t(at, w, x, tl):
    groups, kdim, per = at.shape
    nl = per // tl
    row = lambda g, l: (g * nl + l, 0)
    return pl.pallas_call(
        _proj_residual_t_body,
        grid=(groups, nl),
        in_specs=[
            pl.BlockSpec((1, kdim, tl), lambda g, l: (g, 0, l)),
            pl.BlockSpec(w.shape, lambda g, l: (0, 0)),
            pl.BlockSpec((tl, w.shape[1]), row),
        ],
        out_specs=pl.BlockSpec((tl, w.shape[1]), row),
        out_shape=jax.ShapeDtypeStruct(x.shape, F32),
        compiler_params=_params("arbitrary", "arbitrary"),
        name="proj_residual",
    )(at, w, x)


def _whole(shape):
    return pl.BlockSpec(shape, lambda: (0,) * len(shape))


def _call_whole(body, name, out_shapes, *args):
    return pl.pallas_call(
        body,
        in_specs=[_whole(a.shape) for a in args],
        out_specs=[_whole(s.shape) for s in out_shapes],
        out_shape=out_shapes,
        compiler_params=pltpu.CompilerParams(vmem_limit_bytes=VMEM_LIMIT),
        name=name,
    )(*args)


def _sconv_sample_body(x_ref, nw_ref, win_ref, cw_ref, wout_ref, p0_ref, p1_ref, o_ref, g_ref):
    x = x_ref[...]
    h = _rms(x, nw_ref[...]).astype(BF16)
    p = jnp.dot(h, win_ref[...], preferred_element_type=F32)
    g = p[:, D_MODEL:2 * D_MODEL] * p[:, 2 * D_MODEL:]
    c = _conv_step(g, cw_ref[...], [p0_ref[...], p1_ref[...]])
    o_ref[...] = x + _bdot(p[:, :D_MODEL] * c, wout_ref[...])
    g_ref[...] = g


def _sconv_sample(x, nw, w_in, conv_w, w_out, past):
    sds = jax.ShapeDtypeStruct((DEC_BATCH, D_MODEL), F32)
    x_new, g = _call_whole(_sconv_sample_body, "sconv_sample", [sds, sds],
                           x, nw, w_in, conv_w, w_out, past[:, 0], past[:, 1])
    return x_new, jnp.stack([past[:, 1], g], axis=1)


def _ffn_sample_body(x_ref, nw_ref, wup_ref, cw_ref, cb_ref, wdn_ref, fnw_ref, p0_ref, p1_ref,
                     o_ref, u_ref, *, final):
    x = x_ref[...]
    h = _rms(x, nw_ref[...]).astype(BF16)
    acc = x
    for c in range(D_FF // FFN_CHUNK):
        halves = []
        for base in (0, D_FF):
            lo = base + c * FFN_CHUNK
            hi = lo + FFN_CHUNK
            u = jnp.dot(h, wup_ref[:, lo:hi], preferred_element_type=F32)
            u_ref[:, lo:hi] = u
            halves.append(_conv_step(u, cw_ref[:, lo:hi], [p0_ref[:, lo:hi], p1_ref[:, lo:hi]])
                          + cb_ref[:, lo:hi])
        a, g = halves
        acc = acc + _bdot(_silu(g) * a, wdn_ref[c * FFN_CHUNK:(c + 1) * FFN_CHUNK, :])
    if final:
        acc = _rms(acc, fnw_ref[...])
    o_ref[...] = acc


def _ffn_sample(x, nw, w_up, conv_w, conv_b, w_down, fnw, past, final):
    outs = [jax.ShapeDtypeStruct((DEC_BATCH, D_MODEL), F32),
            jax.ShapeDtypeStruct((DEC_BATCH, 2 * D_FF), F32)]
    x_new, u = _call_whole(functools.partial(_ffn_sample_body, final=final), "ffn_sample", outs,
                           x, nw, w_up, conv_w, conv_b, w_down, fnw, past[:, 0], past[:, 1])
    return x_new, jnp.stack([past[:, 1], u], axis=1)


def _ssd_inproj_sample_body(x_ref, nw_ref, wz_ref, wx_ref, wdt_ref, cw_ref, cb_ref, dtb_ref,
                            p0_ref, p1_ref, p2_ref, z_ref, xbc_ref, dt_ref, u_ref):
    h = _rms(x_ref[...], nw_ref[...]).astype(BF16)
    z_ref[...] = jnp.dot(h, wz_ref[...], preferred_element_type=F32)
    dt_ref[...] = _softplus(jnp.dot(h, wdt_ref[...], preferred_element_type=F32) + dtb_ref[...])
    u = jnp.dot(h, wx_ref[...], preferred_element_type=F32)
    u_ref[...] = u
    conv = _conv_step(u, cw_ref[...], [p0_ref[...], p1_ref[...], p2_ref[...]])
    xbc_ref[...] = _silu(conv + cb_ref[...])


def _ssd_state_sample_body(xbc_ref, dt_ref, alog_ref, dskip_ref, st_ref, y_ref, nst_ref):
    n = SSD_STATE
    xbc = xbc_ref[0]
    dt = dt_ref[0]
    e_dec = jnp.exp(dt * (-jnp.exp(alog_ref[...])))
    row = lax.broadcasted_iota(jnp.int32, (n, n), 0)
    col = lax.broadcasted_iota(jnp.int32, (n, n), 1)
    row_lo = row < SSD_HEAD_DIM
    for j in range(SSD_PAIRS):
        g = j // (SSD_PAIRS // SSD_GROUPS)
        h0 = 2 * j
        b_row = xbc[:, D_INNER + g * n:D_INNER + (g + 1) * n]
        c_row = xbc[:, D_INNER + (SSD_GROUPS + g) * n:D_INNER + (SSD_GROUPS + g + 1) * n]
        xs = xbc[:, j * LANES:(j + 1) * LANES]
        xdt = xs * _pair_lanes(dt, h0, (1, LANES))
        state = st_ref[0, j]
        y = jnp.sum(c_row * b_row, axis=1, keepdims=True) * xdt
        c_rows = jnp.broadcast_to(c_row, (SUBLANES, n))
        y = y + _bdot_nt(c_rows, state)[0:1, :] * _pair_lanes(e_dec, h0, (1, LANES))
        y_ref[0, :, j * LANES:(j + 1) * LANES] = y + xs * dskip_ref[:, j * LANES:(j + 1) * LANES]
        diag = jnp.where(row == col, jnp.broadcast_to(xdt, (n, n)), 0.0)
        upd = _bdot(diag, jnp.broadcast_to(b_row, (n, n)))
        scale = jnp.where(row_lo, e_dec[:, h0:h0 + 1], e_dec[:, h0 + 1:h0 + 2])
        nst_ref[0, j] = state * scale + upd


def _ssd_out_sample_body(x_ref, y_ref, z_ref, nw_ref, wout_ref, o_ref):
    yn = _group_norm_gate(y_ref[...], z_ref[...], nw_ref[...])
    o_ref[...] = x_ref[...] + _bdot(yn, wout_ref[...])


def _ssd_sample(x, nw, w_z, w_x, w_dt, conv_w, conv_b, dt_bias, a_log, d_skip, norm_w, w_out,
                state, conv_past):
    r = DEC_BATCH
    outs = [jax.ShapeDtypeStruct((r, D_INNER), F32), jax.ShapeDtypeStruct((r, SSD_CONV_DIM), F32),
            jax.ShapeDtypeStruct((r, LANES), F32), jax.ShapeDtypeStruct((r, SSD_CONV_DIM), F32)]
    z, xbc, dt, u = _call_whole(_ssd_inproj_sample_body, "ssd_inproj_sample", outs,
                                x, nw, w_z, w_x, w_dt, conv_w, conv_b, dt_bias,
                                conv_past[:, 0], conv_past[:, 1], conv_past[:, 2])
    new_conv = jnp.concatenate([conv_past[:, 1:], u[:, None]], axis=1)
    st_shape = (r, SSD_PAIRS, 2 * SSD_HEAD_DIM, SSD_STATE)
    st_spec = pl.BlockSpec((1,) + st_shape[1:], lambda b: (b, 0, 0, 0))
    vec = lambda w: pl.BlockSpec((1, 1, w), lambda b: (b, 0, 0))
    const = lambda w: pl.BlockSpec((1, w), lambda b: (0, 0))
    y, new_state = pl.pallas_call(
        _ssd_state_sample_body,
        grid=(r,),
        in_specs=[vec(SSD_CONV_DIM), vec(LANES), const(LANES), const(D_INNER), st_spec],
        out_specs=[vec(D_INNER), st_spec],
        out_shape=[jax.ShapeDtypeStruct((r, 1, D_INNER), F32), jax.ShapeDtypeStruct(st_shape, F32)],
        compiler_params=_params("arbitrary"),
        name="ssd_state_sample",
    )(xbc.reshape(r, 1, SSD_CONV_DIM), dt.reshape(r, 1, LANES), a_log, d_skip,
      state.reshape(st_shape))
    (x_new,) = _call_whole(_ssd_out_sample_body, "ssd_out_sample",
                           [jax.ShapeDtypeStruct((r, D_MODEL), F32)],
                           x, y.reshape(r, D_INNER), z, norm_w, w_out)
    return x_new, new_conv, new_state.reshape(r, SSD_HEADS, SSD_HEAD_DIM, SSD_STATE)


def _qkv_sample_body(x_ref, nw_ref, wt_ref, cos_ref, sdn_ref, sup_ref, qt_ref, kt_ref, vt_ref, sown_ref):
    rows = x_ref.shape[0]
    tab = lambda r: jnp.broadcast_to(r[...], (ATTN_HEAD_DIM, rows))
    qt, kt, vt = _qkv_t(x_ref[...], nw_ref[...], wt_ref, tab(cos_ref), tab(sdn_ref), tab(sup_ref))
    qt_ref[...] = qt
    kt_ref[...] = kt
    vt_ref[...] = vt
    per_head = (qt * kt).reshape(ATTN_HEADS, ATTN_HEAD_DIM, rows)
    sown_ref[...] = jnp.sum(per_head, axis=1, keepdims=True) * ATTN_SCALE


N_PAST_BLK = PAST_LEN // MOBA_BLOCK
PAGES_PER_BLK = MOBA_BLOCK // PAGE_SIZE


STEP_PAGES = 4
STEP_BLKS = STEP_PAGES // PAGES_PER_BLK
K_STEPS = N_PAGES // STEP_PAGES
HEAD3 = (ATTN_HEADS, ATTN_HEAD_DIM, LANES)


def _moba_sample_body(pt_ref, qt_ref, vnt_ref, sown_ref, *refs):
    del pt_ref
    k_refs, v_refs = refs[:STEP_PAGES], refs[STEP_PAGES:2 * STEP_PAGES]
    ot_ref, qb_ref, vnb_ref, s_ref, gate_ref, p_ref, acc_ref = refs[2 * STEP_PAGES:]
    b = pl.program_id(0)
    n = pl.program_id(1)
    blk = MOBA_BLOCK
    is_seq = lax.broadcasted_iota(jnp.int32, (D_MODEL, LANES), 1) == b
    blk_lane = lax.broadcasted_iota(jnp.int32, gate_ref.shape, 2)

    @pl.when((b == 0) & (n == 0))
    def _():
        ot_ref[...] = jnp.zeros_like(ot_ref)

    @pl.when(n == 0)
    def _():
        for src, dst in ((qt_ref, qb_ref), (vnt_ref, vnb_ref)):
            col = jnp.sum(jnp.where(is_seq, src[...], 0.0), axis=1, keepdims=True)
            dst[...] = jnp.broadcast_to(col, (D_MODEL, LANES)).reshape(HEAD3)
        gate_ref[...] = jnp.zeros_like(gate_ref)

    @pl.when(n < K_STEPS)
    def _():
        qb = qb_ref[...]
        for j in range(STEP_BLKS):
            raw_sum = jnp.zeros((ATTN_HEADS, 1, 1), F32)
            for w in range(PAGES_PER_BLK):
                page = (n * STEP_BLKS + j) * PAGES_PER_BLK + w
                s = jnp.sum(k_refs[j * PAGES_PER_BLK + w][0] * qb, axis=1, keepdims=True)
                s_ref[:, :, pl.ds(pl.multiple_of(page * PAGE_SIZE, PAGE_SIZE), PAGE_SIZE)] = s * ATTN_SCALE
                raw_sum = raw_sum + jnp.sum(s, axis=2, keepdims=True)
            gate_ref[...] = jnp.where(blk_lane == n * STEP_BLKS + j, raw_sum / blk, gate_ref[...])

    @pl.when(n == K_STEPS)
    def _():
        gate = gate_ref[...]
        s_own = jnp.sum(jnp.where(blk_lane == b, sown_ref[...], 0.0), axis=2, keepdims=True)
        m = s_own
        masked = []
        for j in range(N_PAST_BLK):
            sel = _moba_select(gate, j, N_PAST_BLK)
            sb = jnp.where(sel, s_ref[:, :, j * blk:(j + 1) * blk], NEG_INF)
            masked.append(sb)
            m = jnp.maximum(m, jnp.max(sb, axis=2, keepdims=True))
        p_own = jnp.exp(s_own - m)
        l = p_own
        ps = []
        for sb in masked:
            pb = jnp.exp(sb - m)
            ps.append(pb)
            l = l + jnp.sum(pb, axis=2, keepdims=True)
        for j in range(N_PAST_BLK):
            p_ref[:, :, j * blk:(j + 1) * blk] = ps[j] / l
        first_lane = lax.broadcasted_iota(jnp.int32, HEAD3, 2) == 0
        acc_ref[...] = jnp.where(first_lane, (p_own / l) * vnb_ref[...], 0.0)

    @pl.when(n >= K_STEPS)
    def _():
        acc = acc_ref[...]
        for w in range(STEP_PAGES):
            page = (n - K_STEPS) * STEP_PAGES + w
            p = p_ref[:, :, pl.ds(pl.multiple_of(page * PAGE_SIZE, PAGE_SIZE), PAGE_SIZE)]
            acc = acc + p * v_refs[w][0]
        acc_ref[...] = acc

    @pl.when(n == pl.num_programs(1) - 1)
    def _():
        o_col = jnp.sum(acc_ref[...], axis=2, keepdims=True)
        o_full = jnp.broadcast_to(o_col, HEAD3).reshape(D_MODEL, LANES)
        ot_ref[...] = jnp.where(is_seq, o_full, ot_ref[...])


def _moba_sample(qt, vnt, s_own, cache_kt, cache_vt, page_table):
    r = DEC_BATCH
    whole2 = pl.BlockSpec((D_MODEL, r), lambda b, n, pt: (0, 0))

    def page(w, is_v):
        def index(b, n, pt):
            step = jnp.clip(n - K_STEPS, 0, K_STEPS - 1) if is_v else jnp.minimum(n, K_STEPS - 1)
            return (pt[b * N_PAGES + step * STEP_PAGES + w], 0, 0, 0)
        return pl.BlockSpec((1,) + HEAD3, index)

    row3 = lambda width: pltpu.VMEM((ATTN_HEADS, 1, width), F32)
    grid_spec = pltpu.PrefetchScalarGridSpec(
        num_scalar_prefetch=1,
        grid=(r, 2 * K_STEPS),
        in_specs=([whole2, whole2, pl.BlockSpec((ATTN_HEADS, 1, r), lambda b, n, pt: (0, 0, 0))]
                  + [page(w, False) for w in range(STEP_PAGES)]
                  + [page(w, True) for w in range(STEP_PAGES)]),
        out_specs=whole2,
        scratch_shapes=[
            pltpu.VMEM(HEAD3, F32),
            pltpu.VMEM(HEAD3, F32),
            row3(PAST_LEN),
            row3(LANES),
            row3(PAST_LEN),
            pltpu.VMEM(HEAD3, F32),
        ],
    )
    return pl.pallas_call(
        _moba_sample_body,
        grid_spec=grid_spec,
        out_shape=jax.ShapeDtypeStruct((D_MODEL, r), F32),
        compiler_params=_params("arbitrary", "arbitrary"),
        name="moba_sample",
    )(page_table.reshape(-1), qt, vnt, s_own, *([cache_kt] * STEP_PAGES), *([cache_vt] * STEP_PAGES))


def _rope_tables(pos):
    half = ROT_DIM // 2
    inv_freq = ROPE_THETA ** (-(jnp.arange(half, dtype=F32) * 2.0) / ROT_DIM)
    ang = pos.astype(F32)[:, None] * inv_freq
    cos, sin = jnp.cos(ang), jnp.sin(ang)
    ones = jnp.ones((pos.shape[0], ATTN_HEAD_DIM - ROT_DIM), F32)
    zeros = jnp.zeros((pos.shape[0], ATTN_HEAD_DIM - ROT_DIM), F32)
    zh = jnp.zeros_like(sin)
    return (jnp.concatenate([cos, cos, ones], axis=1),
            jnp.concatenate([-sin, zh, zeros], axis=1),
            jnp.concatenate([zh, sin, zeros], axis=1))


def _pad_lanes(a):
    return jnp.pad(a, [(0, 0)] * (a.ndim - 1) + [(0, LANES - a.shape[-1])])


def kernel(x_prompt, x_sample, state_sconv, state_ssm, state_ssm_conv, cache_k, cache_v, state_ffn_conv, page_table, norm_mix_w, norm_ffn_w, norm_final_w, sconv_w_in, sconv_conv_w, sconv_w_out, ssd_w_in, ssd_conv_w, ssd_conv_b, ssd_dt_bias, ssd_a_log, ssd_d, ssd_norm_w, ssd_w_out, attn_w_qkv, attn_w_o, ffn_w_up, ffn_conv_w, ffn_conv_b, ffn_w_down):
    xp = x_prompt.reshape(N_TOK, D_MODEL)
    xs = x_sample.reshape(DEC_BATCH, D_MODEL)
    fnw = norm_final_w.reshape(1, D_MODEL)
    sconv_p, sconv_s, ffnc_p, ffnc_s = [], [], [], []
    for i in range(DEPTH):
        kind, j = i % N_MIXERS, i // N_MIXERS
        nw = norm_mix_w[i].reshape(1, D_MODEL)
        if kind == 0:
            w_in, w_out = sconv_w_in[j].astype(BF16), sconv_w_out[j].astype(BF16)
            xp, st = _sconv_prompt(xp, nw, w_in, sconv_conv_w[j], w_out)
            xs, st_s = _sconv_sample(xs, nw, w_in, sconv_conv_w[j], w_out, state_sconv[j])
            sconv_p.append(st)
            sconv_s.append(st_s)
        elif kind == 1:
            w_in = ssd_w_in[j]
            w_z = w_in[:, :D_INNER].astype(BF16)
            w_x = w_in[:, D_INNER:D_INNER + SSD_CONV_DIM].astype(BF16)
            w_dt = _pad_lanes(w_in[:, D_INNER + SSD_CONV_DIM:]).astype(BF16)
            w_out = ssd_w_out[j].astype(BF16)
            conv_b = ssd_conv_b[j].reshape(1, SSD_CONV_DIM)
            dt_bias = _pad_lanes(ssd_dt_bias[j].reshape(1, SSD_HEADS))
            a_log = _pad_lanes(ssd_a_log[j].reshape(1, SSD_HEADS))
            d_skip = jnp.repeat(ssd_d[j], SSD_HEAD_DIM).reshape(1, D_INNER)
            norm_w = ssd_norm_w[j].reshape(1, D_INNER)
            z, xbc, dt, ssmc_p = _ssd_inproj_prompt(xp, nw, w_z, w_x, w_dt, ssd_conv_w[j], conv_b, dt_bias)
            xp, ssm_p = _ssd_scan_prompt(xp, z, xbc, dt, a_log, d_skip, norm_w, w_out)
            ssm_p = ssm_p.reshape(BATCH, SSD_HEADS, SSD_HEAD_DIM, SSD_STATE)
            xs, ssmc_s, ssm_s = _ssd_sample(xs, nw, w_z, w_x, w_dt, ssd_conv_w[j], conv_b, dt_bias,
                                            a_log, d_skip, norm_w, w_out, state_ssm[j], state_ssm_conv[j])
        else:
            w_qkv, w_o = attn_w_qkv[j].astype(BF16), attn_w_o[j].astype(BF16)
            w_qkv_t = w_qkv.T
            tabs = [t.T for t in _rope_tables(jnp.arange(SEQ))]
            qt, kt_p, vt_p, k_rows = _qkv_prompt(xp, nw, w_qkv_t, *tabs)
            xp = _proj_residual_t(_moba_prompt(qt, k_rows, vt_p), w_o, xp, PROJ_TL)
            tabs = [t.T for t in _rope_tables(jnp.full((1,), PAST_LEN))]
            sds = jax.ShapeDtypeStruct((D_MODEL, DEC_BATCH), F32)
            own = jax.ShapeDtypeStruct((ATTN_HEADS, 1, DEC_BATCH), F32)
            qt_s, kt_s, vt_s, s_own = _call_whole(_qkv_sample_body, "qkv_sample", [sds, sds, sds, own],
                                                  xs, nw, w_qkv_t, *tabs)
            pages = lambda c: jnp.transpose(c, (0, 2, 3, 1))
            ot_s = _moba_sample(qt_s, vt_s, s_own, pages(cache_k[j]), pages(cache_v[j]), page_table)
            xs = _proj_residual_t(ot_s[None], w_o, xs, DEC_BATCH)
        nwf = norm_ffn_w[i].reshape(1, D_MODEL)
        w_up, w_down = ffn_w_up[i].astype(BF16), ffn_w_down[i].astype(BF16)
        conv_b = ffn_conv_b[i].reshape(1, 2 * D_FF)
        final = i == DEPTH - 1
        xp, fc_p = _ffn_prompt(xp, nwf, w_up, ffn_conv_w[i], conv_b, w_down, fnw, final)
        xs, fc_s = _ffn_sample(xs, nwf, w_up, ffn_conv_w[i], conv_b, w_down, fnw, state_ffn_conv[i], final)
        ffnc_p.append(fc_p)
        ffnc_s.append(fc_s)
    heads = (ATTN_HEADS, ATTN_HEAD_DIM)
    kv_p = lambda t: jnp.transpose(t.reshape((BATCH,) + heads + (SEQ,)), (0, 3, 1, 2))[None]
    kv_s = lambda t: jnp.transpose(t.reshape(heads + (DEC_BATCH,)), (2, 0, 1))[None, :, None]
    return (xp.reshape(BATCH, SEQ, D_MODEL), xs.reshape(DEC_BATCH, 1, D_MODEL),
            jnp.stack(sconv_p), jnp.stack(sconv_s),
            ssm_p[None], ssm_s[None], ssmc_p[None], ssmc_s[None],
            kv_p(kt_p), kv_p(vt_p), kv_s(kt_s), kv_s(vt_s),
            jnp.stack(ffnc_p), jnp.stack(ffnc_s))
```

```python
import functools

import jax
import jax.numpy as jnp
from jax import lax
from jax.experimental import pallas as pl
from jax.experimental.pallas import tpu as pltpu

F32 = jnp.float32
BF16 = jnp.bfloat16

D_MODEL = 1024
BATCH = 8
SEQ = 2048
DEPTH = 4
DEC_BATCH = 128
PAST_LEN = 2048
PAGE_SIZE = 128
N_PAGES = PAST_LEN // PAGE_SIZE
N_MIXERS = 3
RMS_EPS = 1e-6
SCONV_WIDTH = 3
D_INNER = 2 * D_MODEL
SSD_HEAD_DIM = 64
SSD_HEADS = D_INNER // SSD_HEAD_DIM
SSD_GROUPS = 4
SSD_STATE = 128
SSD_CONV_WIDTH = 4
SSD_CONV_DIM = D_INNER + 2 * SSD_GROUPS * SSD_STATE
SSD_CHUNK = 128
SSD_PAIRS = SSD_HEADS // 2
ATTN_HEAD_DIM = 64
ATTN_HEADS = D_MODEL // ATTN_HEAD_DIM
MOBA_BLOCK = 256
MOBA_TOP_K = 3
ROPE_THETA = 500000.0
ROT_DIM = ATTN_HEAD_DIM // 4
D_FF = 2816
FFN_CONV_WIDTH = 3

LANES = 128
SUBLANES = 8
VMEM_LIMIT = 56 * 1024 * 1024
N_TOK = BATCH * SEQ
NEG_INF = float("-inf")


def _params(*sem):
    return pltpu.CompilerParams(dimension_semantics=sem, vmem_limit_bytes=VMEM_LIMIT)


def _bdot(a, b):
    return jnp.dot(a.astype(BF16), b.astype(BF16), preferred_element_type=F32)


def _bdot_nt(a, b):
    return lax.dot_general(a.astype(BF16), b.astype(BF16), (((1,), (1,)), ((), ())),
                           preferred_element_type=F32)


def _rms(x, w):
    inv = lax.rsqrt(jnp.mean(x * x, axis=-1, keepdims=True) + RMS_EPS)
    return (x * inv) * w


def _silu(x):
    return x * (1.0 / (1.0 + jnp.exp(-x)))


def _softplus(x):
    return jnp.maximum(x, 0.0) + jnp.log1p(jnp.exp(-jnp.abs(x)))


def _conv_rows(u, w, prev):
    width = w.shape[0]
    row8 = lax.broadcasted_iota(jnp.int32, (SUBLANES, u.shape[1]), 0)
    out = u * w[width - 1:width, :]
    for k in range(1, width):
        sh = pltpu.roll(u, k, axis=0)
        head = sh[:SUBLANES]
        for t in range(k):
            src = SUBLANES - k + t
            head = jnp.where(row8 == t, prev[src:src + 1, :], head)
        sh = jnp.concatenate([head, sh[SUBLANES:]], axis=0)
        out = out + sh * w[width - 1 - k:width - k, :]
    return out


def _conv_step(u, w, past):
    width = w.shape[0]
    out = u * w[width - 1:width, :]
    for k in range(width - 1):
        out = out + past[k] * w[k:k + 1, :]
    return out


SCONV_TL = 1024


def _sconv_prompt_body(x_ref, nw_ref, win_ref, cw_ref, wout_ref, o_ref, st_ref, carry_ref):
    l = pl.program_id(1)

    @pl.when(l == 0)
    def _():
        carry_ref[...] = jnp.zeros_like(carry_ref)

    x = x_ref[...]
    h = _rms(x, nw_ref[...]).astype(BF16)
    p = jnp.dot(h, win_ref[...], preferred_element_type=F32)
    g = p[:, D_MODEL:2 * D_MODEL] * p[:, 2 * D_MODEL:]
    c = _conv_rows(g, cw_ref[...], carry_ref[...])
    carry_ref[...] = g[SCONV_TL - SUBLANES:, :]
    o_ref[...] = x + _bdot(p[:, :D_MODEL] * c, wout_ref[...])

    @pl.when(l == pl.num_programs(1) - 1)
    def _():
        st_ref[0] = g[SCONV_TL - (SCONV_WIDTH - 1):, :]


def _sconv_prompt(x, nw, w_in, conv_w, w_out):
    nl = SEQ // SCONV_TL
    row = lambda b, l: (b * nl + l, 0)
    const = lambda b, l: (0, 0)
    return pl.pallas_call(
        _sconv_prompt_body,
        grid=(BATCH, nl),
        in_specs=[
            pl.BlockSpec((SCONV_TL, D_MODEL), row),
            pl.BlockSpec((1, D_MODEL), const),
            pl.BlockSpec((D_MODEL, 3 * D_MODEL), const),
            pl.BlockSpec((SCONV_WIDTH, D_MODEL), const),
            pl.BlockSpec((D_MODEL, D_MODEL), const),
        ],
        out_specs=[
            pl.BlockSpec((SCONV_TL, D_MODEL), row),
            pl.BlockSpec((1, SCONV_WIDTH - 1, D_MODEL), lambda b, l: (b, 0, 0)),
        ],
        out_shape=[
            jax.ShapeDtypeStruct((N_TOK, D_MODEL), F32),
            jax.ShapeDtypeStruct((BATCH, SCONV_WIDTH - 1, D_MODEL), F32),
        ],
        scratch_shapes=[pltpu.VMEM((SUBLANES, D_MODEL), F32)],
        compiler_params=_params("arbitrary", "arbitrary"),
        name="sconv_prompt",
    )(x, nw, w_in, conv_w, w_out)


FFN_TL = 1024
FFN_CHUNK = D_FF // 2


def _ffn_prompt_body(x_ref, nw_ref, wup_ref, cw_ref, cb_ref, wdn_ref, fnw_ref, o_ref, st_ref,
                     carry_ref, *, final):
    l = pl.program_id(1)

    @pl.when(l == 0)
    def _():
        carry_ref[...] = jnp.zeros_like(carry_ref)

    x = x_ref[...]
    h = _rms(x, nw_ref[...]).astype(BF16)
    acc = x
    last = l == pl.num_programs(1) - 1
    for c in range(D_FF // FFN_CHUNK):
        halves = []
        for base in (0, D_FF):
            lo = base + c * FFN_CHUNK
            hi = lo + FFN_CHUNK
            u = jnp.dot(h, wup_ref[:, lo:hi], preferred_element_type=F32)
            conv = _conv_rows(u, cw_ref[:, lo:hi], carry_ref[:, lo:hi]) + cb_ref[:, lo:hi]
            carry_ref[:, lo:hi] = u[FFN_TL - SUBLANES:, :]

            @pl.when(last)
            def _(u=u, lo=lo, hi=hi):
                st_ref[0, :, lo:hi] = u[FFN_TL - (FFN_CONV_WIDTH - 1):, :]

            halves.append(conv)
        a, g = halves
        acc = acc + _bdot(_silu(g) * a, wdn_ref[c * FFN_CHUNK:(c + 1) * FFN_CHUNK, :])
    if final:
        acc = _rms(acc, fnw_ref[...])
    o_ref[...] = acc


def _ffn_prompt(x, nw, w_up, conv_w, conv_b, w_down, fnw, final):
    nl = SEQ // FFN_TL
    row = lambda b, l: (b * nl + l, 0)
    const = lambda b, l: (0, 0)
    once = pl.Buffered(1)
    return pl.pallas_call(
        functools.partial(_ffn_prompt_body, final=final),
        grid=(BATCH, nl),
        in_specs=[
            pl.BlockSpec((FFN_TL, D_MODEL), row),
            pl.BlockSpec((1, D_MODEL), const),
            pl.BlockSpec((D_MODEL, 2 * D_FF), const, pipeline_mode=once),
            pl.BlockSpec((FFN_CONV_WIDTH, 2 * D_FF), const),
            pl.BlockSpec((1, 2 * D_FF), const),
            pl.BlockSpec((D_FF, D_MODEL), const, pipeline_mode=once),
            pl.BlockSpec((1, D_MODEL), const),
        ],
        out_specs=[
            pl.BlockSpec((FFN_TL, D_MODEL), row),
            pl.BlockSpec((1, FFN_CONV_WIDTH - 1, 2 * D_FF), lambda b, l: (b, 0, 0)),
        ],
        out_shape=[
            jax.ShapeDtypeStruct((N_TOK, D_MODEL), F32),
            jax.ShapeDtypeStruct((BATCH, FFN_CONV_WIDTH - 1, 2 * D_FF), F32),
        ],
        scratch_shapes=[pltpu.VMEM((SUBLANES, 2 * D_FF), F32)],
        compiler_params=_params("arbitrary", "arbitrary"),
        name="ffn_prompt",
    )(x, nw, w_up, conv_w, conv_b, w_down, fnw)


SSD_TL = 256


def _ssd_inproj_prompt_body(x_ref, nw_ref, wz_ref, wx_ref, wdt_ref, cw_ref, cb_ref, dtb_ref,
                            z_ref, xbc_ref, dt_ref, st_ref, carry_ref):
    l = pl.program_id(1)

    @pl.when(l == 0)
    def _():
        carry_ref[...] = jnp.zeros_like(carry_ref)

    h = _rms(x_ref[...], nw_ref[...]).astype(BF16)
    z_ref[...] = jnp.dot(h, wz_ref[...], preferred_element_type=F32)
    dt_ref[...] = _softplus(jnp.dot(h, wdt_ref[...], preferred_element_type=F32) + dtb_ref[...])
    u = jnp.dot(h, wx_ref[...], preferred_element_type=F32)
    xbc_ref[...] = _silu(_conv_rows(u, cw_ref[...], carry_ref[...]) + cb_ref[...])
    carry_ref[...] = u[SSD_TL - SUBLANES:, :]

    @pl.when(l == pl.num_programs(1) - 1)
    def _():
        st_ref[0] = u[SSD_TL - (SSD_CONV_WIDTH - 1):, :]


def _ssd_inproj_prompt(x, nw, w_z, w_x, w_dt, conv_w, conv_b, dt_bias):
    nl = SEQ // SSD_TL
    row = lambda b, l: (b * nl + l, 0)
    const = lambda b, l: (0, 0)
    return pl.pallas_call(
        _ssd_inproj_prompt_body,
        grid=(BATCH, nl),
        in_specs=[
            pl.BlockSpec((SSD_TL, D_MODEL), row),
            pl.BlockSpec((1, D_MODEL), const),
            pl.BlockSpec((D_MODEL, D_INNER), const),
            pl.BlockSpec((D_MODEL, SSD_CONV_DIM), const),
            pl.BlockSpec((D_MODEL, LANES), const),
            pl.BlockSpec((SSD_CONV_WIDTH, SSD_CONV_DIM), const),
            pl.BlockSpec((1, SSD_CONV_DIM), const),
            pl.BlockSpec((1, LANES), const),
        ],
        out_specs=[
            pl.BlockSpec((SSD_TL, D_INNER), row),
            pl.BlockSpec((SSD_TL, SSD_CONV_DIM), row),
            pl.BlockSpec((SSD_TL, LANES), row),
            pl.BlockSpec((1, SSD_CONV_WIDTH - 1, SSD_CONV_DIM), lambda b, l: (b, 0, 0)),
        ],
        out_shape=[
            jax.ShapeDtypeStruct((N_TOK, D_INNER), F32),
            jax.ShapeDtypeStruct((N_TOK, SSD_CONV_DIM), F32),
            jax.ShapeDtypeStruct((N_TOK, LANES), F32),
            jax.ShapeDtypeStruct((BATCH, SSD_CONV_WIDTH - 1, SSD_CONV_DIM), F32),
        ],
        scratch_shapes=[pltpu.VMEM((SUBLANES, SSD_CONV_DIM), F32)],
        compiler_params=_params("arbitrary", "arbitrary"),
        name="ssd_inproj_prompt",
    )(x, nw, w_z, w_x, w_dt, conv_w, conv_b, dt_bias)


def _cumsum_rows(x):
    n = x.shape[0]
    row = lax.broadcasted_iota(jnp.int32, x.shape, 0)
    k = 1
    while k < n:
        x = x + jnp.where(row >= k, pltpu.roll(x, k, axis=0), 0.0)
        k *= 2
    return x


def _pair_lanes(vals, h0, shape):
    lane = lax.broadcasted_iota(jnp.int32, shape, 1)
    return jnp.where(lane < SSD_HEAD_DIM, vals[:, h0:h0 + 1], vals[:, h0 + 1:h0 + 2])


def _group_norm_gate(y, z, nw):
    y = y * _silu(z)
    gw = D_INNER // SSD_GROUPS
    outs = []
    for g in range(SSD_GROUPS):
        yg = y[:, g * gw:(g + 1) * gw]
        outs.append(_rms(yg, nw[:, g * gw:(g + 1) * gw]))
    return jnp.concatenate(outs, axis=1)


def _ssd_scan_prompt_body(x_ref, z_ref, xbc_ref, dt_ref, alog_ref, dskip_ref, nw_ref, wout_ref,
                          o_ref, st_ref, y_ref):
    c = pl.program_id(1)

    @pl.when(c == 0)
    def _():
        st_ref[...] = jnp.zeros_like(st_ref)

    q = SSD_CHUNK
    p2 = 2 * SSD_HEAD_DIM
    dt = dt_ref[...]
    acum = _cumsum_rows(dt * (-jnp.exp(alog_ref[...])))
    last = acum[q - 1:q, :]
    acum_t = acum.T
    dt_t = dt.T
    w_t = (dt * jnp.exp(last - acum)).T
    e_acum_t = jnp.exp(acum_t)
    e_chunk = jnp.exp(last)
    row = lax.broadcasted_iota(jnp.int32, (q, q), 0)
    col = lax.broadcasted_iota(jnp.int32, (q, q), 1)
    causal_t = col >= row
    row_lo = lax.broadcasted_iota(jnp.int32, (p2, q), 0) < SSD_HEAD_DIM

    def head_rows(vals_t, h0):
        return jnp.where(row_lo, vals_t[h0:h0 + 1, :], vals_t[h0 + 1:h0 + 2, :])

    for g in range(SSD_GROUPS):
        b_g = xbc_ref[:, D_INNER + g * SSD_STATE:D_INNER + (g + 1) * SSD_STATE].astype(BF16)
        c_g = xbc_ref[:, D_INNER + (SSD_GROUPS + g) * SSD_STATE:D_INNER + (SSD_GROUPS + g + 1) * SSD_STATE]
        cb_t = _bdot_nt(b_g, c_g)
        c_t = c_g.T.astype(BF16)
        for j in range(g * SSD_PAIRS // SSD_GROUPS, (g + 1) * SSD_PAIRS // SSD_GROUPS):
            h0 = 2 * j
            xs_t = xbc_ref[:, j * LANES:(j + 1) * LANES].T
            xdt_t = (xs_t * head_rows(dt_t, h0)).astype(BF16)
            yd = []
            for hh in range(2):
                h = h0 + hh
                seg = acum_t[h:h + 1, :] - acum[:, h:h + 1]
                decay = jnp.exp(jnp.where(causal_t, seg, NEG_INF))
                yd.append(_bdot(xdt_t[hh * SSD_HEAD_DIM:(hh + 1) * SSD_HEAD_DIM, :], cb_t * decay))
            state = st_ref[0, j]
            y_t = jnp.concatenate(yd, axis=0) + _bdot(state, c_t) * head_rows(e_acum_t, h0)
            y_t = y_t + xs_t * dskip_ref[j * p2:(j + 1) * p2, :]
            y_ref[:, j * LANES:(j + 1) * LANES] = y_t.T
            upd = _bdot(xs_t * head_rows(w_t, h0), b_g)
            scale = jnp.where(row_lo, e_chunk[:, h0:h0 + 1], e_chunk[:, h0 + 1:h0 + 2])
            st_ref[0, j] = state * scale + upd

    yn = _group_norm_gate(y_ref[...], z_ref[...], nw_ref[...])
    o_ref[...] = x_ref[...] + _bdot(yn, wout_ref[...])


def _ssd_scan_prompt(x, z, xbc, dt, a_log, d_skip, nw, w_out):
    nc = SEQ // SSD_CHUNK
    row = lambda b, c: (b * nc + c, 0)
    const = lambda b, c: (0, 0)
    return pl.pallas_call(
        _ssd_scan_prompt_body,
        grid=(BATCH, nc),
        in_specs=[
            pl.BlockSpec((SSD_CHUNK, D_MODEL), row),
            pl.BlockSpec((SSD_CHUNK, D_INNER), row),
            pl.BlockSpec((SSD_CHUNK, SSD_CONV_DIM), row),
            pl.BlockSpec((SSD_CHUNK, LANES), row),
            pl.BlockSpec((1, LANES), const),
            pl.BlockSpec((D_INNER, LANES), const),
            pl.BlockSpec((1, D_INNER), const),
            pl.BlockSpec((D_INNER, D_MODEL), const),
        ],
        out_specs=[
            pl.BlockSpec((SSD_CHUNK, D_MODEL), row),
            pl.BlockSpec((1, SSD_PAIRS, 2 * SSD_HEAD_DIM, SSD_STATE), lambda b, c: (b, 0, 0, 0)),
        ],
        out_shape=[
            jax.ShapeDtypeStruct((N_TOK, D_MODEL), F32),
            jax.ShapeDtypeStruct((BATCH, SSD_PAIRS, 2 * SSD_HEAD_DIM, SSD_STATE), F32),
        ],
        scratch_shapes=[pltpu.VMEM((SSD_CHUNK, D_INNER), F32)],
        compiler_params=_params("arbitrary", "arbitrary"),
        name="ssd_scan_prompt",
    )(x, z, xbc, dt, a_log, d_skip, nw, w_out)


QKV_TL = 512


def _rope_t(t, cos, sin_dn, sin_up):
    reps = t.shape[0] // ATTN_HEAD_DIM
    half = ROT_DIM // 2
    tile = lambda a: jnp.concatenate([a] * reps, axis=0)
    return (t * tile(cos)
            + pltpu.roll(t, t.shape[0] - half, axis=0) * tile(sin_dn)
            + pltpu.roll(t, half, axis=0) * tile(sin_up))


def _qkv_t(x, nw, wt_ref, cos, sdn, sup):
    ht = _rms(x, nw).T.astype(BF16)
    qt = _rope_t(jnp.dot(wt_ref[:D_MODEL, :], ht, preferred_element_type=F32), cos, sdn, sup)
    kt = _rope_t(jnp.dot(wt_ref[D_MODEL:2 * D_MODEL, :], ht, preferred_element_type=F32), cos, sdn, sup)
    vt = jnp.dot(wt_ref[2 * D_MODEL:, :], ht, preferred_element_type=F32)
    return qt, kt, vt


def _qkv_prompt_body(x_ref, nw_ref, wt_ref, cos_ref, sdn_ref, sup_ref, qt_ref, kt_ref, vt_ref, k_ref):
    qt, kt, vt = _qkv_t(x_ref[...], nw_ref[...], wt_ref, cos_ref[...], sdn_ref[...], sup_ref[...])
    qt_ref[0] = qt
    kt_ref[0] = kt
    vt_ref[0] = vt
    k_ref[...] = kt.T


def _qkv_prompt(x, nw, w_qkv_t, cos, sdn, sup):
    nl = SEQ // QKV_TL
    row = lambda b, l: (b * nl + l, 0)
    const = lambda b, l: (0, 0)
    tab = pl.BlockSpec((ATTN_HEAD_DIM, QKV_TL), lambda b, l: (0, l))
    tspec = pl.BlockSpec((1, D_MODEL, QKV_TL), lambda b, l: (b, 0, l))
    tout = jax.ShapeDtypeStruct((BATCH, D_MODEL, SEQ), F32)
    return pl.pallas_call(
        _qkv_prompt_body,
        grid=(BATCH, nl),
        in_specs=[
            pl.BlockSpec((QKV_TL, D_MODEL), row),
            pl.BlockSpec((1, D_MODEL), const),
            pl.BlockSpec((3 * D_MODEL, D_MODEL), const),
            tab, tab, tab,
        ],
        out_specs=[tspec, tspec, tspec, pl.BlockSpec((QKV_TL, D_MODEL), row)],
        out_shape=[tout, tout, tout, jax.ShapeDtypeStruct((N_TOK, D_MODEL), F32)],
        compiler_params=_params("arbitrary", "arbitrary"),
        name="qkv_prompt",
    )(x, nw, w_qkv_t, cos, sdn, sup)


N_BLK = SEQ // MOBA_BLOCK
ATTN_SCALE = ATTN_HEAD_DIM ** -0.5


def _moba_select(gate, n, own):
    ax = gate.ndim - 1
    lane = lax.broadcasted_iota(jnp.int32, gate.shape, ax)
    g_n = jnp.sum(jnp.where(lane == n, gate, 0.0), axis=ax, keepdims=True)
    ahead = (gate > g_n) | ((gate == g_n) & (lane < n))
    rank = jnp.sum(jnp.where(ahead & (lane < own), 1.0, 0.0), axis=ax, keepdims=True)
    return rank < MOBA_TOP_K


def _moba_bias_rows(gate_t, own):
    row = lax.broadcasted_iota(jnp.int32, gate_t.shape, 0)
    past = row < own
    bias = jnp.full(gate_t.shape, NEG_INF, F32)
    for n in range(gate_t.shape[0] - 1):
        g_n = gate_t[n:n + 1, :]
        ahead = (gate_t > g_n) | ((gate_t == g_n) & (row < n))
        rank = jnp.sum(jnp.where(ahead & past, 1.0, 0.0), axis=0, keepdims=True)
        chosen = (rank < MOBA_TOP_K) & (row == n) & past
        bias = jnp.where(chosen, 0.0, bias)
    return bias


def _dot3(a, b):
    a_hi = a.astype(BF16)
    b_hi = b.astype(BF16)
    a_lo = (a - a_hi.astype(F32)).astype(BF16)
    b_lo = (b - b_hi.astype(F32)).astype(BF16)
    return (jnp.dot(a_hi, b_hi, preferred_element_type=F32)
            + jnp.dot(a_hi, b_lo, preferred_element_type=F32)
            + jnp.dot(a_lo, b_hi, preferred_element_type=F32))


MOBA_HEADS = 4
MOBA_ROWS = MOBA_HEADS * ATTN_HEAD_DIM


def _moba_prompt_body(qt_ref, k_ref, vt_ref, ot_ref, kmean_ref, bias_ref):
    i = pl.program_id(2)
    blk = MOBA_BLOCK
    dh = ATTN_HEAD_DIM
    heads = range(MOBA_HEADS)

    @pl.when(i == 0)
    def _():
        for n in range(N_BLK):
            kmean_ref[n:n + 1, :] = jnp.mean(k_ref[n * blk:(n + 1) * blk, :], axis=0, keepdims=True)

    qt = qt_ref[0]
    q_row = lax.broadcasted_iota(jnp.int32, qt.shape, 0)
    km = kmean_ref[...]
    km_lane = lax.broadcasted_iota(jnp.int32, km.shape, 1)
    qs = qt * ATTN_SCALE
    q_w = []
    for hh in heads:
        q_w.append(jnp.where((q_row >= hh * dh) & (q_row < (hh + 1) * dh), qs, 0.0).astype(BF16))
        km_h = jnp.where((km_lane >= hh * dh) & (km_lane < (hh + 1) * dh), km, 0.0)
        bias_ref[hh] = _moba_bias_rows(_dot3(km_h, qt), i)

    key = lax.broadcasted_iota(jnp.int32, (blk, blk), 0)
    qry = lax.broadcasted_iota(jnp.int32, (blk, blk), 1)

    def scores(n):
        k_n = k_ref[pl.ds(pl.multiple_of(n * blk, blk), blk), :].astype(BF16)
        return [jnp.dot(k_n, q_w[hh], preferred_element_type=F32) for hh in heads]

    def absorb(n, masked, state):
        vt_n = vt_ref[0, :, pl.ds(pl.multiple_of(n * blk, blk), blk)].astype(BF16)
        out = []
        for hh in heads:
            m, l, acc = state[3 * hh:3 * hh + 3]
            s = masked[hh]
            m_new = jnp.maximum(m, jnp.max(s, axis=0, keepdims=True))
            m_ref = jnp.where(m_new == NEG_INF, 0.0, m_new)
            alpha = jnp.exp(m - m_ref)
            p = jnp.exp(s - m_ref)
            l = l * alpha + jnp.sum(p, axis=0, keepdims=True)
            acc = acc * alpha + jnp.dot(vt_n[hh * dh:(hh + 1) * dh, :], p.astype(BF16),
                                        preferred_element_type=F32)
            out += [m_new, l, acc]
        return out

    n_state = 3 * MOBA_HEADS

    def past_block(n, carry):
        nxt = scores(n + 1)
        masked = [carry[n_state + hh] + bias_ref[hh, pl.ds(n, 1), :] for hh in heads]
        return tuple(absorb(n, masked, carry[:n_state]) + nxt)

    init = (jnp.full((1, blk), NEG_INF, F32), jnp.zeros((1, blk), F32), jnp.zeros((dh, blk), F32))
    carry = lax.fori_loop(0, i, past_block, init * MOBA_HEADS + tuple(scores(0)))
    own = [jnp.where(key <= qry, carry[n_state + hh], NEG_INF) for hh in heads]
    final = absorb(i, own, carry[:n_state])
    ot_ref[0] = jnp.concatenate([final[3 * hh + 2] / final[3 * hh + 1] for hh in heads], axis=0)


def _moba_prompt(qt, k, vt):
    groups = ATTN_HEADS // MOBA_HEADS
    qspec = pl.BlockSpec((1, MOBA_ROWS, MOBA_BLOCK), lambda b, p, i: (b, p, i))
    return pl.pallas_call(
        _moba_prompt_body,
        grid=(BATCH, groups, N_BLK),
        in_specs=[
            qspec,
            pl.BlockSpec((SEQ, MOBA_ROWS), lambda b, p, i: (b, p)),
            pl.BlockSpec((1, MOBA_ROWS, SEQ), lambda b, p, i: (b, p, 0)),
        ],
        out_specs=qspec,
        out_shape=jax.ShapeDtypeStruct((BATCH, D_MODEL, SEQ), F32),
        scratch_shapes=[pltpu.VMEM((N_BLK, MOBA_ROWS), F32),
                        pltpu.VMEM((MOBA_HEADS, N_BLK, MOBA_BLOCK), F32)],
        compiler_params=_params("arbitrary", "arbitrary", "arbitrary"),
        name="moba_prompt",
    )(qt, k, vt)


PROJ_TL = 512


def _proj_residual_t_body(at_ref, w_ref, x_ref, o_ref):
    o_ref[...] = x_ref[...] + _bdot(at_ref[0].T, w_ref[...])


def _proj_residual_t(at, w, x, tl):
    groups, kdim, per = at.shape
    nl = per // tl
    row = lambda g, l: (g * nl + l, 0)
    return pl.pallas_call(
        _proj_residual_t_body,
        grid=(groups, nl),
        in_specs=[
            pl.BlockSpec((1, kdim, tl), lambda g, l: (g, 0, l)),
            pl.BlockSpec(w.shape, lambda g, l: (0, 0)),
            pl.BlockSpec((tl, w.shape[1]), row),
        ],
        out_specs=pl.BlockSpec((tl, w.shape[1]), row),
        out_shape=jax.ShapeDtypeStruct(x.shape, F32),
        compiler_params=_params("arbitrary", "arbitrary"),
        name="proj_residual",
    )(at, w, x)


def _whole(shape):
    return pl.BlockSpec(shape, lambda: (0,) * len(shape))


def _call_whole(body, name, out_shapes, *args):
    return pl.pallas_call(
        body,
        in_specs=[_whole(a.shape) for a in args],
        out_specs=[_whole(s.shape) for s in out_shapes],
        out_shape=out_shapes,
        compiler_params=pltpu.CompilerParams(vmem_limit_bytes=VMEM_LIMIT),
        name=name,
    )(*args)


def _sconv_sample_body(x_ref, nw_ref, win_ref, cw_ref, wout_ref, p0_ref, p1_ref, o_ref, g_ref):
    x = x_ref[...]
    h = _rms(x, nw_ref[...]).astype(BF16)
    p = jnp.dot(h, win_ref[...], preferred_element_type=F32)
    g = p[:, D_MODEL:2 * D_MODEL] * p[:, 2 * D_MODEL:]
    c = _conv_step(g, cw_ref[...], [p0_ref[...], p1_ref[...]])
    o_ref[...] = x + _bdot(p[:, :D_MODEL] * c, wout_ref[...])
    g_ref[...] = g


def _sconv_sample(x, nw, w_in, conv_w, w_out, past):
    sds = jax.ShapeDtypeStruct((DEC_BATCH, D_MODEL), F32)
    x_new, g = _call_whole(_sconv_sample_body, "sconv_sample", [sds, sds],
                           x, nw, w_in, conv_w, w_out, past[:, 0], past[:, 1])
    return x_new, jnp.stack([past[:, 1], g], axis=1)


def _ffn_sample_body(x_ref, nw_ref, wup_ref, cw_ref, cb_ref, wdn_ref, fnw_ref, p0_ref, p1_ref,
                     o_ref, u_ref, *, final):
    x = x_ref[...]
    h = _rms(x, nw_ref[...]).astype(BF16)
    acc = x
    for c in range(D_FF // FFN_CHUNK):
        halves = []
        for base in (0, D_FF):
            lo = base + c * FFN_CHUNK
            hi = lo + FFN_CHUNK
            u = jnp.dot(h, wup_ref[:, lo:hi], preferred_element_type=F32)
            u_ref[:, lo:hi] = u
            halves.append(_conv_step(u, cw_ref[:, lo:hi], [p0_ref[:, lo:hi], p1_ref[:, lo:hi]])
                          + cb_ref[:, lo:hi])
        a, g = halves
        acc = acc + _bdot(_silu(g) * a, wdn_ref[c * FFN_CHUNK:(c + 1) * FFN_CHUNK, :])
    if final:
        acc = _rms(acc, fnw_ref[...])
    o_ref[...] = acc


def _ffn_sample(x, nw, w_up, conv_w, conv_b, w_down, fnw, past, final):
    outs = [jax.ShapeDtypeStruct((DEC_BATCH, D_MODEL), F32),
            jax.ShapeDtypeStruct((DEC_BATCH, 2 * D_FF), F32)]
    x_new, u = _call_whole(functools.partial(_ffn_sample_body, final=final), "ffn_sample", outs,
                           x, nw, w_up, conv_w, conv_b, w_down, fnw, past[:, 0], past[:, 1])
    return x_new, jnp.stack([past[:, 1], u], axis=1)


def _ssd_inproj_sample_body(x_ref, nw_ref, wz_ref, wx_ref, wdt_ref, cw_ref, cb_ref, dtb_ref,
                            p0_ref, p1_ref, p2_ref, z_ref, xbc_ref, dt_ref, u_ref):
    h = _rms(x_ref[...], nw_ref[...]).astype(BF16)
    z_ref[...] = jnp.dot(h, wz_ref[...], preferred_element_type=F32)
    dt_ref[...] = _softplus(jnp.dot(h, wdt_ref[...], preferred_element_type=F32) + dtb_ref[...])
    u = jnp.dot(h, wx_ref[...], preferred_element_type=F32)
    u_ref[...] = u
    conv = _conv_step(u, cw_ref[...], [p0_ref[...], p1_ref[...], p2_ref[...]])
    xbc_ref[...] = _silu(conv + cb_ref[...])


def _ssd_state_sample_body(xbc_ref, dt_ref, alog_ref, dskip_ref, st_ref, y_ref, nst_ref):
    n = SSD_STATE
    xbc = xbc_ref[0]
    dt = dt_ref[0]
    e_dec = jnp.exp(dt * (-jnp.exp(alog_ref[...])))
    row = lax.broadcasted_iota(jnp.int32, (n, n), 0)
    col = lax.broadcasted_iota(jnp.int32, (n, n), 1)
    row_lo = row < SSD_HEAD_DIM
    for j in range(SSD_PAIRS):
        g = j // (SSD_PAIRS // SSD_GROUPS)
        h0 = 2 * j
        b_row = xbc[:, D_INNER + g * n:D_INNER + (g + 1) * n]
        c_row = xbc[:, D_INNER + (SSD_GROUPS + g) * n:D_INNER + (SSD_GROUPS + g + 1) * n]
        xs = xbc[:, j * LANES:(j + 1) * LANES]
        xdt = xs * _pair_lanes(dt, h0, (1, LANES))
        state = st_ref[0, j]
        y = jnp.sum(c_row * b_row, axis=1, keepdims=True) * xdt
        c_rows = jnp.broadcast_to(c_row, (SUBLANES, n))
        y = y + _bdot_nt(c_rows, state)[0:1, :] * _pair_lanes(e_dec, h0, (1, LANES))
        y_ref[0, :, j * LANES:(j + 1) * LANES] = y + xs * dskip_ref[:, j * LANES:(j + 1) * LANES]
        diag = jnp.where(row == col, jnp.broadcast_to(xdt, (n, n)), 0.0)
        upd = _bdot(diag, jnp.broadcast_to(b_row, (n, n)))
        scale = jnp.where(row_lo, e_dec[:, h0:h0 + 1], e_dec[:, h0 + 1:h0 + 2])
        nst_ref[0, j] = state * scale + upd


def _ssd_out_sample_body(x_ref, y_ref, z_ref, nw_ref, wout_ref, o_ref):
    yn = _group_norm_gate(y_ref[...], z_ref[...], nw_ref[...])
    o_ref[...] = x_ref[...] + _bdot(yn, wout_ref[...])


def _ssd_sample(x, nw, w_z, w_x, w_dt, conv_w, conv_b, dt_bias, a_log, d_skip, norm_w, w_out,
                state, conv_past):
    r = DEC_BATCH
    outs = [jax.ShapeDtypeStruct((r, D_INNER), F32), jax.ShapeDtypeStruct((r, SSD_CONV_DIM), F32),
            jax.ShapeDtypeStruct((r, LANES), F32), jax.ShapeDtypeStruct((r, SSD_CONV_DIM), F32)]
    z, xbc, dt, u = _call_whole(_ssd_inproj_sample_body, "ssd_inproj_sample", outs,
                                x, nw, w_z, w_x, w_dt, conv_w, conv_b, dt_bias,
                                conv_past[:, 0], conv_past[:, 1], conv_past[:, 2])
    new_conv = jnp.concatenate([conv_past[:, 1:], u[:, None]], axis=1)
    st_shape = (r, SSD_PAIRS, 2 * SSD_HEAD_DIM, SSD_STATE)
    st_spec = pl.BlockSpec((1,) + st_shape[1:], lambda b: (b, 0, 0, 0))
    vec = lambda w: pl.BlockSpec((1, 1, w), lambda b: (b, 0, 0))
    const = lambda w: pl.BlockSpec((1, w), lambda b: (0, 0))
    y, new_state = pl.pallas_call(
        _ssd_state_sample_body,
        grid=(r,),
        in_specs=[vec(SSD_CONV_DIM), vec(LANES), const(LANES), const(D_INNER), st_spec],
        out_specs=[vec(D_INNER), st_spec],
        out_shape=[jax.ShapeDtypeStruct((r, 1, D_INNER), F32), jax.ShapeDtypeStruct(st_shape, F32)],
        compiler_params=_params("arbitrary"),
        name="ssd_state_sample",
    )(xbc.reshape(r, 1, SSD_CONV_DIM), dt.reshape(r, 1, LANES), a_log, d_skip,
      state.reshape(st_shape))
    (x_new,) = _call_whole(_ssd_out_sample_body, "ssd_out_sample",
                           [jax.ShapeDtypeStruct((r, D_MODEL), F32)],
                           x, y.reshape(r, D_INNER), z, norm_w, w_out)
    return x_new, new_conv, new_state.reshape(r, SSD_HEADS, SSD_HEAD_DIM, SSD_STATE)


def _qkv_sample_body(x_ref, nw_ref, wt_ref, cos_ref, sdn_ref, sup_ref, qt_ref, kt_ref, vt_ref, sown_ref):
    rows = x_ref.shape[0]
    tab = lambda r: jnp.broadcast_to(r[...], (ATTN_HEAD_DIM, rows))
    qt, kt, vt = _qkv_t(x_ref[...], nw_ref[...], wt_ref, tab(cos_ref), tab(sdn_ref), tab(sup_ref))
    qt_ref[...] = qt
    kt_ref[...] = kt
    vt_ref[...] = vt
    per_head = (qt * kt).reshape(ATTN_HEADS, ATTN_HEAD_DIM, rows)
    sown_ref[...] = jnp.sum(per_head, axis=1) * ATTN_SCALE


N_PAST_BLK = PAST_LEN // MOBA_BLOCK
PAGES_PER_BLK = MOBA_BLOCK // PAGE_SIZE
STEP_PAGES = 8
STEP_BLKS = STEP_PAGES // PAGES_PER_BLK
K_STEPS = N_PAGES // STEP_PAGES
HEAD3 = (ATTN_HEADS, ATTN_HEAD_DIM, LANES)


def _moba_sample_body(pt_ref, qt_ref, vnt_ref, sown_ref, *refs):
    del pt_ref
    k_refs, v_refs = refs[:STEP_PAGES], refs[STEP_PAGES:2 * STEP_PAGES]
    ot_ref, qb_ref, vnb_ref, s_ref, gate_ref, p_ref, acc_ref = refs[2 * STEP_PAGES:]
    b = pl.program_id(0)
    n = pl.program_id(1)
    blk = MOBA_BLOCK
    is_seq = lax.broadcasted_iota(jnp.int32, (D_MODEL, LANES), 1) == b
    lane = lax.broadcasted_iota(jnp.int32, gate_ref.shape, 1)

    @pl.when((b == 0) & (n == 0))
    def _():
        ot_ref[...] = jnp.zeros_like(ot_ref)

    @pl.when(n == 0)
    def _():
        for src, dst in ((qt_ref, qb_ref), (vnt_ref, vnb_ref)):
            col = jnp.sum(jnp.where(is_seq, src[...], 0.0), axis=1, keepdims=True)
            dst[...] = jnp.broadcast_to(col, (D_MODEL, LANES)).reshape(HEAD3)
        gate_ref[...] = jnp.zeros_like(gate_ref)

    @pl.when(n < K_STEPS)
    def _():
        qb = qb_ref[...]
        for j in range(STEP_BLKS):
            raw_sum = jnp.zeros((ATTN_HEADS, 1), F32)
            for w in range(PAGES_PER_BLK):
                page = (n * STEP_BLKS + j) * PAGES_PER_BLK + w
                s = jnp.sum(k_refs[j * PAGES_PER_BLK + w][0] * qb, axis=1)
                s_ref[:, pl.ds(pl.multiple_of(page * PAGE_SIZE, PAGE_SIZE), PAGE_SIZE)] = s * ATTN_SCALE
                raw_sum = raw_sum + jnp.sum(s, axis=1, keepdims=True)
            gate_ref[...] = jnp.where(lane == n * STEP_BLKS + j, raw_sum / blk, gate_ref[...])

    @pl.when(n == K_STEPS)
    def _():
        gate = gate_ref[...]
        s_own = jnp.sum(jnp.where(lane == b, sown_ref[...], 0.0), axis=1, keepdims=True)
        m = s_own
        masked = []
        for j in range(N_PAST_BLK):
            sel = _moba_select(gate, j, N_PAST_BLK)
            sb = jnp.where(sel, s_ref[:, j * blk:(j + 1) * blk], NEG_INF)
            masked.append(sb)
            m = jnp.maximum(m, jnp.max(sb, axis=1, keepdims=True))
        p_own = jnp.exp(s_own - m)
        l = p_own
        ps = []
        for sb in masked:
            pb = jnp.exp(sb - m)
            ps.append(pb)
            l = l + jnp.sum(pb, axis=1, keepdims=True)
        for j in range(N_PAST_BLK):
            p_ref[:, j * blk:(j + 1) * blk] = ps[j] / l
        first_lane = lax.broadcasted_iota(jnp.int32, HEAD3, 2) == 0
        w_own = jnp.broadcast_to(p_own / l, (ATTN_HEADS, LANES)).reshape(ATTN_HEADS, 1, LANES)
        acc_ref[...] = jnp.where(first_lane, w_own * vnb_ref[...], 0.0)

    @pl.when(n >= K_STEPS)
    def _():
        acc = acc_ref[...]
        for w in range(STEP_PAGES):
            page = (n - K_STEPS) * STEP_PAGES + w
            p = p_ref[:, pl.ds(pl.multiple_of(page * PAGE_SIZE, PAGE_SIZE), PAGE_SIZE)]
            acc = acc + p.reshape(ATTN_HEADS, 1, PAGE_SIZE) * v_refs[w][0]
        acc_ref[...] = acc

    @pl.when(n == pl.num_programs(1) - 1)
    def _():
        o_col = jnp.sum(acc_ref[...], axis=2, keepdims=True)
        o_full = jnp.broadcast_to(o_col, HEAD3).reshape(D_MODEL, LANES)
        ot_ref[...] = jnp.where(is_seq, o_full, ot_ref[...])


def _moba_sample(qt, vnt, s_own, cache_kt, cache_vt, page_table):
    r = DEC_BATCH
    whole2 = pl.BlockSpec((D_MODEL, r), lambda b, n, pt: (0, 0))

    def page(w, is_v):
        def index(b, n, pt):
            step = jnp.clip(n - K_STEPS, 0, K_STEPS - 1) if is_v else jnp.minimum(n, K_STEPS - 1)
            return (pt[b * N_PAGES + step * STEP_PAGES + w], 0, 0, 0)
        return pl.BlockSpec((1,) + HEAD3, index)

    per_head = lambda width: pltpu.VMEM((ATTN_HEADS, width), F32)
    grid_spec = pltpu.PrefetchScalarGridSpec(
        num_scalar_prefetch=1,
        grid=(r, 2 * K_STEPS),
        in_specs=([whole2, whole2, pl.BlockSpec((ATTN_HEADS, r), lambda b, n, pt: (0, 0))]
                  + [page(w, False) for w in range(STEP_PAGES)]
                  + [page(w, True) for w in range(STEP_PAGES)]),
        out_specs=whole2,
        scratch_shapes=[
            pltpu.VMEM(HEAD3, F32),
            pltpu.VMEM(HEAD3, F32),
            per_head(PAST_LEN),
            per_head(LANES),
            per_head(PAST_LEN),
            pltpu.VMEM(HEAD3, F32),
        ],
    )
    return pl.pallas_call(
        _moba_sample_body,
        grid_spec=grid_spec,
        out_shape=jax.ShapeDtypeStruct((D_MODEL, r), F32),
        compiler_params=_params("arbitrary", "arbitrary"),
        name="moba_sample",
    )(page_table.reshape(-1), qt, vnt, s_own, *([cache_kt] * STEP_PAGES), *([cache_vt] * STEP_PAGES))


def _rope_tables(pos):
    half = ROT_DIM // 2
    inv_freq = ROPE_THETA ** (-(jnp.arange(half, dtype=F32) * 2.0) / ROT_DIM)
    ang = pos.astype(F32)[:, None] * inv_freq
    cos, sin = jnp.cos(ang), jnp.sin(ang)
    ones = jnp.ones((pos.shape[0], ATTN_HEAD_DIM - ROT_DIM), F32)
    zeros = jnp.zeros((pos.shape[0], ATTN_HEAD_DIM - ROT_DIM), F32)
    zh = jnp.zeros_like(sin)
    return (jnp.concatenate([cos, cos, ones], axis=1),
            jnp.concatenate([-sin, zh, zeros], axis=1),
            jnp.concatenate([zh, sin, zeros], axis=1))


def _pad_lanes(a):
    return jnp.pad(a, [(0, 0)] * (a.ndim - 1) + [(0, LANES - a.shape[-1])])


def kernel(x_prompt, x_sample, state_sconv, state_ssm, state_ssm_conv, cache_k, cache_v, state_ffn_conv, page_table, norm_mix_w, norm_ffn_w, norm_final_w, sconv_w_in, sconv_conv_w, sconv_w_out, ssd_w_in, ssd_conv_w, ssd_conv_b, ssd_dt_bias, ssd_a_log, ssd_d, ssd_norm_w, ssd_w_out, attn_w_qkv, attn_w_o, ffn_w_up, ffn_conv_w, ffn_conv_b, ffn_w_down):
    xp = x_prompt.reshape(N_TOK, D_MODEL)
    xs = x_sample.reshape(DEC_BATCH, D_MODEL)
    fnw = norm_final_w.reshape(1, D_MODEL)
    sconv_p, sconv_s, ffnc_p, ffnc_s = [], [], [], []
    for i in range(DEPTH):
        kind, j = i % N_MIXERS, i // N_MIXERS
        nw = norm_mix_w[i].reshape(1, D_MODEL)
        if kind == 0:
            w_in, w_out = sconv_w_in[j].astype(BF16), sconv_w_out[j].astype(BF16)
            xp, st = _sconv_prompt(xp, nw, w_in, sconv_conv_w[j], w_out)
            xs, st_s = _sconv_sample(xs, nw, w_in, sconv_conv_w[j], w_out, state_sconv[j])
            sconv_p.append(st)
            sconv_s.append(st_s)
        elif kind == 1:
            w_in = ssd_w_in[j]
            w_z = w_in[:, :D_INNER].astype(BF16)
            w_x = w_in[:, D_INNER:D_INNER + SSD_CONV_DIM].astype(BF16)
            w_dt = _pad_lanes(w_in[:, D_INNER + SSD_CONV_DIM:]).astype(BF16)
            w_out = ssd_w_out[j].astype(BF16)
            conv_b = ssd_conv_b[j].reshape(1, SSD_CONV_DIM)
            dt_bias = _pad_lanes(ssd_dt_bias[j].reshape(1, SSD_HEADS))
            a_log = _pad_lanes(ssd_a_log[j].reshape(1, SSD_HEADS))
            d_skip = jnp.repeat(ssd_d[j], SSD_HEAD_DIM).reshape(1, D_INNER)
            norm_w = ssd_norm_w[j].reshape(1, D_INNER)
            z, xbc, dt, ssmc_p = _ssd_inproj_prompt(xp, nw, w_z, w_x, w_dt, ssd_conv_w[j], conv_b, dt_bias)
            d_skip_rows = jnp.broadcast_to(d_skip.reshape(D_INNER, 1), (D_INNER, LANES))
            xp, ssm_p = _ssd_scan_prompt(xp, z, xbc, dt, a_log, d_skip_rows, norm_w, w_out)
            ssm_p = ssm_p.reshape(BATCH, SSD_HEADS, SSD_HEAD_DIM, SSD_STATE)
            xs, ssmc_s, ssm_s = _ssd_sample(xs, nw, w_z, w_x, w_dt, ssd_conv_w[j], conv_b, dt_bias,
                                            a_log, d_skip, norm_w, w_out, state_ssm[j], state_ssm_conv[j])
        else:
            w_qkv, w_o = attn_w_qkv[j].astype(BF16), attn_w_o[j].astype(BF16)
            w_qkv_t = w_qkv.T
            tabs = [t.T for t in _rope_tables(jnp.arange(SEQ))]
            qt, kt_p, vt_p, k_rows = _qkv_prompt(xp, nw, w_qkv_t, *tabs)
            xp = _proj_residual_t(_moba_prompt(qt, k_rows, vt_p), w_o, xp, PROJ_TL)
            tabs = [t.T for t in _rope_tables(jnp.full((1,), PAST_LEN))]
            sds = jax.ShapeDtypeStruct((D_MODEL, DEC_BATCH), F32)
            own = jax.ShapeDtypeStruct((ATTN_HEADS, DEC_BATCH), F32)
            qt_s, kt_s, vt_s, s_own = _call_whole(_qkv_sample_body, "qkv_sample", [sds, sds, sds, own],
                                                  xs, nw, w_qkv_t, *tabs)
            pages = lambda c: jnp.transpose(c, (0, 2, 3, 1))
            ot_s = _moba_sample(qt_s, vt_s, s_own, pages(cache_k[j]), pages(cache_v[j]), page_table)
            xs = _proj_residual_t(ot_s[None], w_o, xs, DEC_BATCH)
        nwf = norm_ffn_w[i].reshape(1, D_MODEL)
        w_up, w_down = ffn_w_up[i].astype(BF16), ffn_w_down[i].astype(BF16)
        conv_b = ffn_conv_b[i].reshape(1, 2 * D_FF)
        final = i == DEPTH - 1
        xp, fc_p = _ffn_prompt(xp, nwf, w_up, ffn_conv_w[i], conv_b, w_down, fnw, final)
        xs, fc_s = _ffn_sample(xs, nwf, w_up, ffn_conv_w[i], conv_b, w_down, fnw, state_ffn_conv[i], final)
        ffnc_p.append(fc_p)
        ffnc_s.append(fc_s)
    heads = (ATTN_HEADS, ATTN_HEAD_DIM)
    kv_p = lambda t: jnp.transpose(t.reshape((BATCH,) + heads + (SEQ,)), (0, 3, 1, 2))[None]
    kv_s = lambda t: jnp.transpose(t.reshape(heads + (DEC_BATCH,)), (2, 0, 1))[None, :, None]
    return (xp.reshape(BATCH, SEQ, D_MODEL), xs.reshape(DEC_BATCH, 1, D_MODEL),
            jnp.stack(sconv_p), jnp.stack(sconv_s),
            ssm_p[None], ssm_s[None], ssmc_p[None], ssmc_s[None],
            kv_p(kt_p), kv_p(vt_p), kv_s(kt_s), kv_s(vt_s),
            jnp.stack(ffnc_p), jnp.stack(ffnc_s))
```

```python
import functools

import jax
import jax.numpy as jnp
from jax import lax
from jax.experimental import pallas as pl
from jax.experimental.pallas import tpu as pltpu

F32 = jnp.float32
BF16 = jnp.bfloat16

D_MODEL = 1024
BATCH = 8
SEQ = 2048
DEPTH = 4
DEC_BATCH = 128
PAST_LEN = 2048
PAGE_SIZE = 128
N_PAGES = PAST_LEN // PAGE_SIZE
N_MIXERS = 3
RMS_EPS = 1e-6
SCONV_WIDTH = 3
D_INNER = 2 * D_MODEL
SSD_HEAD_DIM = 64
SSD_HEADS = D_INNER // SSD_HEAD_DIM
SSD_GROUPS = 4
SSD_STATE = 128
SSD_CONV_WIDTH = 4
SSD_CONV_DIM = D_INNER + 2 * SSD_GROUPS * SSD_STATE
SSD_CHUNK = 128
SSD_PAIRS = SSD_HEADS // 2
ATTN_HEAD_DIM = 64
ATTN_HEADS = D_MODEL // ATTN_HEAD_DIM
MOBA_BLOCK = 256
MOBA_TOP_K = 3
ROPE_THETA = 500000.0
ROT_DIM = ATTN_HEAD_DIM // 4
D_FF = 2816
FFN_CONV_WIDTH = 3

LANES = 128
SUBLANES = 8
VMEM_LIMIT = 56 * 1024 * 1024
N_TOK = BATCH * SEQ
NEG_INF = float("-inf")


def _params(*sem):
    return pltpu.CompilerParams(dimension_semantics=sem, vmem_limit_bytes=VMEM_LIMIT)


def _bdot(a, b):
    return jnp.dot(a.astype(BF16), b.astype(BF16), preferred_element_type=F32)


def _bdot_nt(a, b):
    return lax.dot_general(a.astype(BF16), b.astype(BF16), (((1,), (1,)), ((), ())),
                           preferred_element_type=F32)


def _rms(x, w):
    inv = lax.rsqrt(jnp.mean(x * x, axis=-1, keepdims=True) + RMS_EPS)
    return (x * inv) * w


def _silu(x):
    return x * (1.0 / (1.0 + jnp.exp(-x)))


def _softplus(x):
    return jnp.maximum(x, 0.0) + jnp.log1p(jnp.exp(-jnp.abs(x)))


def _conv_rows(u, w, prev):
    width = w.shape[0]
    row8 = lax.broadcasted_iota(jnp.int32, (SUBLANES, u.shape[1]), 0)
    out = u * w[width - 1:width, :]
    for k in range(1, width):
        sh = pltpu.roll(u, k, axis=0)
        head = sh[:SUBLANES]
        for t in range(k):
            src = SUBLANES - k + t
            head = jnp.where(row8 == t, prev[src:src + 1, :], head)
        sh = jnp.concatenate([head, sh[SUBLANES:]], axis=0)
        out = out + sh * w[width - 1 - k:width - k, :]
    return out


def _conv_step(u, w, past):
    width = w.shape[0]
    out = u * w[width - 1:width, :]
    for k in range(width - 1):
        out = out + past[k] * w[k:k + 1, :]
    return out


SCONV_TL = 1024


def _sconv_prompt_body(x_ref, nw_ref, win_ref, cw_ref, wout_ref, o_ref, st_ref, carry_ref):
    l = pl.program_id(1)

    @pl.when(l == 0)
    def _():
        carry_ref[...] = jnp.zeros_like(carry_ref)

    x = x_ref[...]
    h = _rms(x, nw_ref[...]).astype(BF16)
    p = jnp.dot(h, win_ref[...], preferred_element_type=F32)
    g = p[:, D_MODEL:2 * D_MODEL] * p[:, 2 * D_MODEL:]
    c = _conv_rows(g, cw_ref[...], carry_ref[...])
    carry_ref[...] = g[SCONV_TL - SUBLANES:, :]
    o_ref[...] = x + _bdot(p[:, :D_MODEL] * c, wout_ref[...])

    @pl.when(l == pl.num_programs(1) - 1)
    def _():
        st_ref[0] = g[SCONV_TL - (SCONV_WIDTH - 1):, :]


def _sconv_prompt(x, nw, w_in, conv_w, w_out):
    nl = SEQ // SCONV_TL
    row = lambda b, l: (b * nl + l, 0)
    const = lambda b, l: (0, 0)
    return pl.pallas_call(
        _sconv_prompt_body,
        grid=(BATCH, nl),
        in_specs=[
            pl.BlockSpec((SCONV_TL, D_MODEL), row),
            pl.BlockSpec((1, D_MODEL), const),
            pl.BlockSpec((D_MODEL, 3 * D_MODEL), const),
            pl.BlockSpec((SCONV_WIDTH, D_MODEL), const),
            pl.BlockSpec((D_MODEL, D_MODEL), const),
        ],
        out_specs=[
            pl.BlockSpec((SCONV_TL, D_MODEL), row),
            pl.BlockSpec((1, SCONV_WIDTH - 1, D_MODEL), lambda b, l: (b, 0, 0)),
        ],
        out_shape=[
            jax.ShapeDtypeStruct((N_TOK, D_MODEL), F32),
            jax.ShapeDtypeStruct((BATCH, SCONV_WIDTH - 1, D_MODEL), F32),
        ],
        scratch_shapes=[pltpu.VMEM((SUBLANES, D_MODEL), F32)],
        compiler_params=_params("arbitrary", "arbitrary"),
        name="sconv_prompt",
    )(x, nw, w_in, conv_w, w_out)


FFN_TL = 1024
FFN_CHUNK = D_FF // 2


def _ffn_prompt_body(x_ref, nw_ref, wup_ref, cw_ref, cb_ref, wdn_ref, fnw_ref, o_ref, st_ref,
                     carry_ref, *, final):
    l = pl.program_id(1)

    @pl.when(l == 0)
    def _():
        carry_ref[...] = jnp.zeros_like(carry_ref)

    x = x_ref[...]
    h = _rms(x, nw_ref[...]).astype(BF16)
    acc = x
    last = l == pl.num_programs(1) - 1
    for c in range(D_FF // FFN_CHUNK):
        halves = []
        for base in (0, D_FF):
            lo = base + c * FFN_CHUNK
            hi = lo + FFN_CHUNK
            u = jnp.dot(h, wup_ref[:, lo:hi], preferred_element_type=F32)
            conv = _conv_rows(u, cw_ref[:, lo:hi], carry_ref[:, lo:hi]) + cb_ref[:, lo:hi]
            carry_ref[:, lo:hi] = u[FFN_TL - SUBLANES:, :]

            @pl.when(last)
            def _(u=u, lo=lo, hi=hi):
                st_ref[0, :, lo:hi] = u[FFN_TL - (FFN_CONV_WIDTH - 1):, :]

            halves.append(conv)
        a, g = halves
        acc = acc + _bdot(_silu(g) * a, wdn_ref[c * FFN_CHUNK:(c + 1) * FFN_CHUNK, :])
    if final:
        acc = _rms(acc, fnw_ref[...])
    o_ref[...] = acc


def _ffn_prompt(x, nw, w_up, conv_w, conv_b, w_down, fnw, final):
    nl = SEQ // FFN_TL
    row = lambda b, l: (b * nl + l, 0)
    const = lambda b, l: (0, 0)
    once = pl.Buffered(1)
    return pl.pallas_call(
        functools.partial(_ffn_prompt_body, final=final),
        grid=(BATCH, nl),
        in_specs=[
            pl.BlockSpec((FFN_TL, D_MODEL), row),
            pl.BlockSpec((1, D_MODEL), const),
            pl.BlockSpec((D_MODEL, 2 * D_FF), const, pipeline_mode=once),
            pl.BlockSpec((FFN_CONV_WIDTH, 2 * D_FF), const),
            pl.BlockSpec((1, 2 * D_FF), const),
            pl.BlockSpec((D_FF, D_MODEL), const, pipeline_mode=once),
            pl.BlockSpec((1, D_MODEL), const),
        ],
        out_specs=[
            pl.BlockSpec((FFN_TL, D_MODEL), row),
            pl.BlockSpec((1, FFN_CONV_WIDTH - 1, 2 * D_FF), lambda b, l: (b, 0, 0)),
        ],
        out_shape=[
            jax.ShapeDtypeStruct((N_TOK, D_MODEL), F32),
            jax.ShapeDtypeStruct((BATCH, FFN_CONV_WIDTH - 1, 2 * D_FF), F32),
        ],
        scratch_shapes=[pltpu.VMEM((SUBLANES, 2 * D_FF), F32)],
        compiler_params=_params("arbitrary", "arbitrary"),
        name="ffn_prompt",
    )(x, nw, w_up, conv_w, conv_b, w_down, fnw)


SSD_TL = 256


def _ssd_inproj_prompt_body(x_ref, nw_ref, wz_ref, wx_ref, wdt_ref, cw_ref, cb_ref, dtb_ref,
                            z_ref, xbc_ref, dt_ref, st_ref, carry_ref):
    l = pl.program_id(1)

    @pl.when(l == 0)
    def _():
        carry_ref[...] = jnp.zeros_like(carry_ref)

    h = _rms(x_ref[...], nw_ref[...]).astype(BF16)
    z_ref[...] = jnp.dot(h, wz_ref[...], preferred_element_type=F32)
    dt_ref[...] = _softplus(jnp.dot(h, wdt_ref[...], preferred_element_type=F32) + dtb_ref[...])
    u = jnp.dot(h, wx_ref[...], preferred_element_type=F32)
    xbc_ref[...] = _silu(_conv_rows(u, cw_ref[...], carry_ref[...]) + cb_ref[...])
    carry_ref[...] = u[SSD_TL - SUBLANES:, :]

    @pl.when(l == pl.num_programs(1) - 1)
    def _():
        st_ref[0] = u[SSD_TL - (SSD_CONV_WIDTH - 1):, :]


def _ssd_inproj_prompt(x, nw, w_z, w_x, w_dt, conv_w, conv_b, dt_bias):
    nl = SEQ // SSD_TL
    row = lambda b, l: (b * nl + l, 0)
    const = lambda b, l: (0, 0)
    return pl.pallas_call(
        _ssd_inproj_prompt_body,
        grid=(BATCH, nl),
        in_specs=[
            pl.BlockSpec((SSD_TL, D_MODEL), row),
            pl.BlockSpec((1, D_MODEL), const),
            pl.BlockSpec((D_MODEL, D_INNER), const),
            pl.BlockSpec((D_MODEL, SSD_CONV_DIM), const),
            pl.BlockSpec((D_MODEL, LANES), const),
            pl.BlockSpec((SSD_CONV_WIDTH, SSD_CONV_DIM), const),
            pl.BlockSpec((1, SSD_CONV_DIM), const),
            pl.BlockSpec((1, LANES), const),
        ],
        out_specs=[
            pl.BlockSpec((SSD_TL, D_INNER), row),
            pl.BlockSpec((SSD_TL, SSD_CONV_DIM), row),
            pl.BlockSpec((SSD_TL, LANES), row),
            pl.BlockSpec((1, SSD_CONV_WIDTH - 1, SSD_CONV_DIM), lambda b, l: (b, 0, 0)),
        ],
        out_shape=[
            jax.ShapeDtypeStruct((N_TOK, D_INNER), F32),
            jax.ShapeDtypeStruct((N_TOK, SSD_CONV_DIM), F32),
            jax.ShapeDtypeStruct((N_TOK, LANES), F32),
            jax.ShapeDtypeStruct((BATCH, SSD_CONV_WIDTH - 1, SSD_CONV_DIM), F32),
        ],
        scratch_shapes=[pltpu.VMEM((SUBLANES, SSD_CONV_DIM), F32)],
        compiler_params=_params("arbitrary", "arbitrary"),
        name="ssd_inproj_prompt",
    )(x, nw, w_z, w_x, w_dt, conv_w, conv_b, dt_bias)


def _cumsum_rows(x):
    n = x.shape[0]
    row = lax.broadcasted_iota(jnp.int32, x.shape, 0)
    k = 1
    while k < n:
        x = x + jnp.where(row >= k, pltpu.roll(x, k, axis=0), 0.0)
        k *= 2
    return x


def _pair_lanes(vals, h0, shape):
    lane = lax.broadcasted_iota(jnp.int32, shape, 1)
    return jnp.where(lane < SSD_HEAD_DIM, vals[:, h0:h0 + 1], vals[:, h0 + 1:h0 + 2])


def _group_norm_gate(y, z, nw):
    y = y * _silu(z)
    gw = D_INNER // SSD_GROUPS
    outs = []
    for g in range(SSD_GROUPS):
        yg = y[:, g * gw:(g + 1) * gw]
        outs.append(_rms(yg, nw[:, g * gw:(g + 1) * gw]))
    return jnp.concatenate(outs, axis=1)


def _ssd_scan_prompt_body(x_ref, z_ref, xbc_ref, dt_ref, alog_ref, dskip_ref, nw_ref, wout_ref,
                          o_ref, st_ref, y_ref):
    c = pl.program_id(1)

    @pl.when(c == 0)
    def _():
        st_ref[...] = jnp.zeros_like(st_ref)

    q = SSD_CHUNK
    p2 = 2 * SSD_HEAD_DIM
    dt = dt_ref[...]
    acum = _cumsum_rows(dt * (-jnp.exp(alog_ref[...])))
    last = acum[q - 1:q, :]
    acum_t = acum.T
    dt_t = dt.T
    w_t = (dt * jnp.exp(last - acum)).T
    e_acum_t = jnp.exp(acum_t)
    e_chunk = jnp.exp(last)
    row = lax.broadcasted_iota(jnp.int32, (q, q), 0)
    col = lax.broadcasted_iota(jnp.int32, (q, q), 1)
    causal_t = col >= row
    row_lo = lax.broadcasted_iota(jnp.int32, (p2, q), 0) < SSD_HEAD_DIM

    def head_rows(vals_t, h0):
        return jnp.where(row_lo, vals_t[h0:h0 + 1, :], vals_t[h0 + 1:h0 + 2, :])

    for g in range(SSD_GROUPS):
        b_g = xbc_ref[:, D_INNER + g * SSD_STATE:D_INNER + (g + 1) * SSD_STATE].astype(BF16)
        c_g = xbc_ref[:, D_INNER + (SSD_GROUPS + g) * SSD_STATE:D_INNER + (SSD_GROUPS + g + 1) * SSD_STATE]
        cb_t = _bdot_nt(b_g, c_g)
        c_t = c_g.T.astype(BF16)
        for j in range(g * SSD_PAIRS // SSD_GROUPS, (g + 1) * SSD_PAIRS // SSD_GROUPS):
            h0 = 2 * j
            xs_t = xbc_ref[:, j * LANES:(j + 1) * LANES].T
            xdt_t = (xs_t * head_rows(dt_t, h0)).astype(BF16)
            yd = []
            for hh in range(2):
                h = h0 + hh
                seg = acum_t[h:h + 1, :] - acum[:, h:h + 1]
                decay = jnp.exp(jnp.where(causal_t, seg, NEG_INF))
                yd.append(_bdot(xdt_t[hh * SSD_HEAD_DIM:(hh + 1) * SSD_HEAD_DIM, :], cb_t * decay))
            state = st_ref[0, j]
            y_t = jnp.concatenate(yd, axis=0) + _bdot(state, c_t) * head_rows(e_acum_t, h0)
            y_t = y_t + xs_t * dskip_ref[j * p2:(j + 1) * p2, :]
            y_ref[:, j * LANES:(j + 1) * LANES] = y_t.T
            upd = _bdot(xs_t * head_rows(w_t, h0), b_g)
            scale = jnp.where(row_lo, e_chunk[:, h0:h0 + 1], e_chunk[:, h0 + 1:h0 + 2])
            st_ref[0, j] = state * scale + upd

    yn = _group_norm_gate(y_ref[...], z_ref[...], nw_ref[...])
    o_ref[...] = x_ref[...] + _bdot(yn, wout_ref[...])


def _ssd_scan_prompt(x, z, xbc, dt, a_log, d_skip, nw, w_out):
    nc = SEQ // SSD_CHUNK
    row = lambda b, c: (b * nc + c, 0)
    const = lambda b, c: (0, 0)
    return pl.pallas_call(
        _ssd_scan_prompt_body,
        grid=(BATCH, nc),
        in_specs=[
            pl.BlockSpec((SSD_CHUNK, D_MODEL), row),
            pl.BlockSpec((SSD_CHUNK, D_INNER), row),
            pl.BlockSpec((SSD_CHUNK, SSD_CONV_DIM), row),
            pl.BlockSpec((SSD_CHUNK, LANES), row),
            pl.BlockSpec((1, LANES), const),
            pl.BlockSpec((D_INNER, LANES), const),
            pl.BlockSpec((1, D_INNER), const),
            pl.BlockSpec((D_INNER, D_MODEL), const),
        ],
        out_specs=[
            pl.BlockSpec((SSD_CHUNK, D_MODEL), row),
            pl.BlockSpec((1, SSD_PAIRS, 2 * SSD_HEAD_DIM, SSD_STATE), lambda b, c: (b, 0, 0, 0)),
        ],
        out_shape=[
            jax.ShapeDtypeStruct((N_TOK, D_MODEL), F32),
            jax.ShapeDtypeStruct((BATCH, SSD_PAIRS, 2 * SSD_HEAD_DIM, SSD_STATE), F32),
        ],
        scratch_shapes=[pltpu.VMEM((SSD_CHUNK, D_INNER), F32)],
        compiler_params=_params("arbitrary", "arbitrary"),
        name="ssd_scan_prompt",
    )(x, z, xbc, dt, a_log, d_skip, nw, w_out)


QKV_TL = 512


def _rope_t(t, cos, sin_dn, sin_up):
    reps = t.shape[0] // ATTN_HEAD_DIM
    half = ROT_DIM // 2
    tile = lambda a: jnp.concatenate([a] * reps, axis=0)
    return (t * tile(cos)
            + pltpu.roll(t, t.shape[0] - half, axis=0) * tile(sin_dn)
            + pltpu.roll(t, half, axis=0) * tile(sin_up))


def _qkv_t(x, nw, wt_ref, cos, sdn, sup):
    ht = _rms(x, nw).T.astype(BF16)
    qt = _rope_t(jnp.dot(wt_ref[:D_MODEL, :], ht, preferred_element_type=F32), cos, sdn, sup)
    kt = _rope_t(jnp.dot(wt_ref[D_MODEL:2 * D_MODEL, :], ht, preferred_element_type=F32), cos, sdn, sup)
    vt = jnp.dot(wt_ref[2 * D_MODEL:, :], ht, preferred_element_type=F32)
    return qt, kt, vt


def _qkv_prompt_body(x_ref, nw_ref, wt_ref, cos_ref, sdn_ref, sup_ref, qt_ref, kt_ref, vt_ref, k_ref):
    qt, kt, vt = _qkv_t(x_ref[...], nw_ref[...], wt_ref, cos_ref[...], sdn_ref[...], sup_ref[...])
    qt_ref[0] = qt
    kt_ref[0] = kt
    vt_ref[0] = vt
    k_ref[...] = kt.T


def _qkv_prompt(x, nw, w_qkv_t, cos, sdn, sup):
    nl = SEQ // QKV_TL
    row = lambda b, l: (b * nl + l, 0)
    const = lambda b, l: (0, 0)
    tab = pl.BlockSpec((ATTN_HEAD_DIM, QKV_TL), lambda b, l: (0, l))
    tspec = pl.BlockSpec((1, D_MODEL, QKV_TL), lambda b, l: (b, 0, l))
    tout = jax.ShapeDtypeStruct((BATCH, D_MODEL, SEQ), F32)
    return pl.pallas_call(
        _qkv_prompt_body,
        grid=(BATCH, nl),
        in_specs=[
            pl.BlockSpec((QKV_TL, D_MODEL), row),
            pl.BlockSpec((1, D_MODEL), const),
            pl.BlockSpec((3 * D_MODEL, D_MODEL), const),
            tab, tab, tab,
        ],
        out_specs=[tspec, tspec, tspec, pl.BlockSpec((QKV_TL, D_MODEL), row)],
        out_shape=[tout, tout, tout, jax.ShapeDtypeStruct((N_TOK, D_MODEL), F32)],
        compiler_params=_params("arbitrary", "arbitrary"),
        name="qkv_prompt",
    )(x, nw, w_qkv_t, cos, sdn, sup)


N_BLK = SEQ // MOBA_BLOCK
ATTN_SCALE = ATTN_HEAD_DIM ** -0.5


def _moba_select(gate, n, own):
    ax = gate.ndim - 1
    lane = lax.broadcasted_iota(jnp.int32, gate.shape, ax)
    g_n = jnp.sum(jnp.where(lane == n, gate, 0.0), axis=ax, keepdims=True)
    ahead = (gate > g_n) | ((gate == g_n) & (lane < n))
    rank = jnp.sum(jnp.where(ahead & (lane < own), 1.0, 0.0), axis=ax, keepdims=True)
    return rank < MOBA_TOP_K


def _moba_bias_rows(gate_t, own):
    row = lax.broadcasted_iota(jnp.int32, gate_t.shape, 0)
    past = row < own
    bias = jnp.full(gate_t.shape, NEG_INF, F32)
    for n in range(gate_t.shape[0] - 1):
        g_n = gate_t[n:n + 1, :]
        ahead = (gate_t > g_n) | ((gate_t == g_n) & (row < n))
        rank = jnp.sum(jnp.where(ahead & past, 1.0, 0.0), axis=0, keepdims=True)
        chosen = (rank < MOBA_TOP_K) & (row == n) & past
        bias = jnp.where(chosen, 0.0, bias)
    return bias


def _dot3(a, b):
    a_hi = a.astype(BF16)
    b_hi = b.astype(BF16)
    a_lo = (a - a_hi.astype(F32)).astype(BF16)
    b_lo = (b - b_hi.astype(F32)).astype(BF16)
    return (jnp.dot(a_hi, b_hi, preferred_element_type=F32)
            + jnp.dot(a_hi, b_lo, preferred_element_type=F32)
            + jnp.dot(a_lo, b_hi, preferred_element_type=F32))


MOBA_HEADS = 4
MOBA_ROWS = MOBA_HEADS * ATTN_HEAD_DIM


def _moba_prompt_body(qt_ref, k_ref, vt_ref, ot_ref, kmean_ref, bias_ref):
    i = pl.program_id(2)
    blk = MOBA_BLOCK
    dh = ATTN_HEAD_DIM
    heads = range(MOBA_HEADS)

    @pl.when(i == 0)
    def _():
        for n in range(N_BLK):
            kmean_ref[n:n + 1, :] = jnp.mean(k_ref[n * blk:(n + 1) * blk, :], axis=0, keepdims=True)

    qt = qt_ref[0]
    q_row = lax.broadcasted_iota(jnp.int32, qt.shape, 0)
    km = kmean_ref[...]
    km_lane = lax.broadcasted_iota(jnp.int32, km.shape, 1)
    qs = qt * ATTN_SCALE
    q_w = []
    for hh in heads:
        q_w.append(jnp.where((q_row >= hh * dh) & (q_row < (hh + 1) * dh), qs, 0.0).astype(BF16))
        km_h = jnp.where((km_lane >= hh * dh) & (km_lane < (hh + 1) * dh), km, 0.0)
        bias_ref[hh] = _moba_bias_rows(_dot3(km_h, qt), i)

    key = lax.broadcasted_iota(jnp.int32, (blk, blk), 0)
    qry = lax.broadcasted_iota(jnp.int32, (blk, blk), 1)

    def scores(n):
        k_n = k_ref[pl.ds(pl.multiple_of(n * blk, blk), blk), :].astype(BF16)
        return [jnp.dot(k_n, q_w[hh], preferred_element_type=F32) for hh in heads]

    def absorb(n, masked, state):
        vt_n = vt_ref[0, :, pl.ds(pl.multiple_of(n * blk, blk), blk)].astype(BF16)
        out = []
        for hh in heads:
            m, l, acc = state[3 * hh:3 * hh + 3]
            s = masked[hh]
            m_new = jnp.maximum(m, jnp.max(s, axis=0, keepdims=True))
            m_ref = jnp.where(m_new == NEG_INF, 0.0, m_new)
            alpha = jnp.exp(m - m_ref)
            p = jnp.exp(s - m_ref)
            l = l * alpha + jnp.sum(p, axis=0, keepdims=True)
            acc = acc * alpha + jnp.dot(vt_n[hh * dh:(hh + 1) * dh, :], p.astype(BF16),
                                        preferred_element_type=F32)
            out += [m_new, l, acc]
        return out

    n_state = 3 * MOBA_HEADS

    def past_block(n, carry):
        nxt = scores(n + 1)
        masked = [carry[n_state + hh] + bias_ref[hh, pl.ds(n, 1), :] for hh in heads]
        return tuple(absorb(n, masked, carry[:n_state]) + nxt)

    init = (jnp.full((1, blk), NEG_INF, F32), jnp.zeros((1, blk), F32), jnp.zeros((dh, blk), F32))
    carry = lax.fori_loop(0, i, past_block, init * MOBA_HEADS + tuple(scores(0)))
    own = [jnp.where(key <= qry, carry[n_state + hh], NEG_INF) for hh in heads]
    final = absorb(i, own, carry[:n_state])
    ot_ref[0] = jnp.concatenate([final[3 * hh + 2] / final[3 * hh + 1] for hh in heads], axis=0)


def _moba_prompt(qt, k, vt):
    groups = ATTN_HEADS // MOBA_HEADS
    qspec = pl.BlockSpec((1, MOBA_ROWS, MOBA_BLOCK), lambda b, p, i: (b, p, i))
    return pl.pallas_call(
        _moba_prompt_body,
        grid=(BATCH, groups, N_BLK),
        in_specs=[
            qspec,
            pl.BlockSpec((SEQ, MOBA_ROWS), lambda b, p, i: (b, p)),
            pl.BlockSpec((1, MOBA_ROWS, SEQ), lambda b, p, i: (b, p, 0)),
        ],
        out_specs=qspec,
        out_shape=jax.ShapeDtypeStruct((BATCH, D_MODEL, SEQ), F32),
        scratch_shapes=[pltpu.VMEM((N_BLK, MOBA_ROWS), F32),
                        pltpu.VMEM((MOBA_HEADS, N_BLK, MOBA_BLOCK), F32)],
        compiler_params=_params("arbitrary", "arbitrary", "arbitrary"),
        name="moba_prompt",
    )(qt, k, vt)


PROJ_TL = 512


def _proj_residual_t_body(at_ref, w_ref, x_ref, o_ref):
    o_ref[...] = x_ref[...] + _bdot(at_ref[0].T, w_ref[...])


def _proj_residual_t(at, w, x, tl):
    groups, kdim, per = at.shape
    nl = per // tl
    row = lambda g, l: (g * nl + l, 0)
    return pl.pallas_call(
        _proj_residual_t_body,
        grid=(groups, nl),
        in_specs=[
            pl.BlockSpec((1, kdim, tl), lambda g, l: (g, 0, l)),
            pl.BlockSpec(w.shape, lambda g, l: (0, 0)),
            pl.BlockSpec((tl, w.shape[1]), row),
        ],
        out_specs=pl.BlockSpec((tl, w.shape[1]), row),
        out_shape=jax.ShapeDtypeStruct(x.shape, F32),
        compiler_params=_params("arbitrary", "arbitrary"),
        name="proj_residual",
    )(at, w, x)


def _whole(shape):
    return pl.BlockSpec(shape, lambda: (0,) * len(shape))


def _call_whole(body, name, out_shapes, *args):
    return pl.pallas_call(
        body,
        in_specs=[_whole(a.shape) for a in args],
        out_specs=[_whole(s.shape) for s in out_shapes],
        out_shape=out_shapes,
        compiler_params=pltpu.CompilerParams(vmem_limit_bytes=VMEM_LIMIT),
        name=name,
    )(*args)


def _sconv_sample_body(x_ref, nw_ref, win_ref, cw_ref, wout_ref, p0_ref, p1_ref, o_ref, g_ref):
    x = x_ref[...]
    h = _rms(x, nw_ref[...]).astype(BF16)
    p = jnp.dot(h, win_ref[...], preferred_element_type=F32)
    g = p[:, D_MODEL:2 * D_MODEL] * p[:, 2 * D_MODEL:]
    c = _conv_step(g, cw_ref[...], [p0_ref[...], p1_ref[...]])
    o_ref[...] = x + _bdot(p[:, :D_MODEL] * c, wout_ref[...])
    g_ref[...] = g


def _sconv_sample(x, nw, w_in, conv_w, w_out, past):
    sds = jax.ShapeDtypeStruct((DEC_BATCH, D_MODEL), F32)
    x_new, g = _call_whole(_sconv_sample_body, "sconv_sample", [sds, sds],
                           x, nw, w_in, conv_w, w_out, past[:, 0], past[:, 1])
    return x_new, jnp.stack([past[:, 1], g], axis=1)


def _ffn_sample_body(x_ref, nw_ref, wup_ref, cw_ref, cb_ref, wdn_ref, fnw_ref, p0_ref, p1_ref,
                     o_ref, u_ref, *, final):
    x = x_ref[...]
    h = _rms(x, nw_ref[...]).astype(BF16)
    acc = x
    for c in range(D_FF // FFN_CHUNK):
        halves = []
        for base in (0, D_FF):
            lo = base + c * FFN_CHUNK
            hi = lo + FFN_CHUNK
            u = jnp.dot(h, wup_ref[:, lo:hi], preferred_element_type=F32)
            u_ref[:, lo:hi] = u
            halves.append(_conv_step(u, cw_ref[:, lo:hi], [p0_ref[:, lo:hi], p1_ref[:, lo:hi]])
                          + cb_ref[:, lo:hi])
        a, g = halves
        acc = acc + _bdot(_silu(g) * a, wdn_ref[c * FFN_CHUNK:(c + 1) * FFN_CHUNK, :])
    if final:
        acc = _rms(acc, fnw_ref[...])
    o_ref[...] = acc


def _ffn_sample(x, nw, w_up, conv_w, conv_b, w_down, fnw, past, final):
    outs = [jax.ShapeDtypeStruct((DEC_BATCH, D_MODEL), F32),
            jax.ShapeDtypeStruct((DEC_BATCH, 2 * D_FF), F32)]
    x_new, u = _call_whole(functools.partial(_ffn_sample_body, final=final), "ffn_sample", outs,
                           x, nw, w_up, conv_w, conv_b, w_down, fnw, past[:, 0], past[:, 1])
    return x_new, jnp.stack([past[:, 1], u], axis=1)


def _ssd_inproj_sample_body(x_ref, nw_ref, wz_ref, wx_ref, wdt_ref, cw_ref, cb_ref, dtb_ref,
                            p0_ref, p1_ref, p2_ref, z_ref, xbc_ref, dt_ref, u_ref):
    h = _rms(x_ref[...], nw_ref[...]).astype(BF16)
    z_ref[...] = jnp.dot(h, wz_ref[...], preferred_element_type=F32)
    dt_ref[...] = _softplus(jnp.dot(h, wdt_ref[...], preferred_element_type=F32) + dtb_ref[...])
    u = jnp.dot(h, wx_ref[...], preferred_element_type=F32)
    u_ref[...] = u
    conv = _conv_step(u, cw_ref[...], [p0_ref[...], p1_ref[...], p2_ref[...]])
    xbc_ref[...] = _silu(conv + cb_ref[...])


def _ssd_state_sample_body(xbc_ref, dt_ref, alog_ref, dskip_ref, st_ref, y_ref, nst_ref):
    n = SSD_STATE
    xbc = xbc_ref[0]
    dt = dt_ref[0]
    e_dec = jnp.exp(dt * (-jnp.exp(alog_ref[...])))
    row = lax.broadcasted_iota(jnp.int32, (n, n), 0)
    col = lax.broadcasted_iota(jnp.int32, (n, n), 1)
    row_lo = row < SSD_HEAD_DIM
    for j in range(SSD_PAIRS):
        g = j // (SSD_PAIRS // SSD_GROUPS)
        h0 = 2 * j
        b_row = xbc[:, D_INNER + g * n:D_INNER + (g + 1) * n]
        c_row = xbc[:, D_INNER + (SSD_GROUPS + g) * n:D_INNER + (SSD_GROUPS + g + 1) * n]
        xs = xbc[:, j * LANES:(j + 1) * LANES]
        xdt = xs * _pair_lanes(dt, h0, (1, LANES))
        state = st_ref[0, j]
        y = jnp.sum(c_row * b_row, axis=1, keepdims=True) * xdt
        c_rows = jnp.broadcast_to(c_row, (SUBLANES, n))
        y = y + _bdot_nt(c_rows, state)[0:1, :] * _pair_lanes(e_dec, h0, (1, LANES))
        y_ref[0, :, j * LANES:(j + 1) * LANES] = y + xs * dskip_ref[:, j * LANES:(j + 1) * LANES]
        diag = jnp.where(row == col, jnp.broadcast_to(xdt, (n, n)), 0.0)
        upd = _bdot(diag, jnp.broadcast_to(b_row, (n, n)))
        scale = jnp.where(row_lo, e_dec[:, h0:h0 + 1], e_dec[:, h0 + 1:h0 + 2])
        nst_ref[0, j] = state * scale + upd


def _ssd_out_sample_body(x_ref, y_ref, z_ref, nw_ref, wout_ref, o_ref):
    yn = _group_norm_gate(y_ref[...], z_ref[...], nw_ref[...])
    o_ref[...] = x_ref[...] + _bdot(yn, wout_ref[...])


def _ssd_sample(x, nw, w_z, w_x, w_dt, conv_w, conv_b, dt_bias, a_log, d_skip, norm_w, w_out,
                state, conv_past):
    r = DEC_BATCH
    outs = [jax.ShapeDtypeStruct((r, D_INNER), F32), jax.ShapeDtypeStruct((r, SSD_CONV_DIM), F32),
            jax.ShapeDtypeStruct((r, LANES), F32), jax.ShapeDtypeStruct((r, SSD_CONV_DIM), F32)]
    z, xbc, dt, u = _call_whole(_ssd_inproj_sample_body, "ssd_inproj_sample", outs,
                                x, nw, w_z, w_x, w_dt, conv_w, conv_b, dt_bias,
                                conv_past[:, 0], conv_past[:, 1], conv_past[:, 2])
    new_conv = jnp.concatenate([conv_past[:, 1:], u[:, None]], axis=1)
    st_shape = (r, SSD_PAIRS, 2 * SSD_HEAD_DIM, SSD_STATE)
    st_spec = pl.BlockSpec((1,) + st_shape[1:], lambda b: (b, 0, 0, 0))
    vec = lambda w: pl.BlockSpec((1, 1, w), lambda b: (b, 0, 0))
    const = lambda w: pl.BlockSpec((1, w), lambda b: (0, 0))
    y, new_state = pl.pallas_call(
        _ssd_state_sample_body,
        grid=(r,),
        in_specs=[vec(SSD_CONV_DIM), vec(LANES), const(LANES), const(D_INNER), st_spec],
        out_specs=[vec(D_INNER), st_spec],
        out_shape=[jax.ShapeDtypeStruct((r, 1, D_INNER), F32), jax.ShapeDtypeStruct(st_shape, F32)],
        compiler_params=_params("arbitrary"),
        name="ssd_state_sample",
    )(xbc.reshape(r, 1, SSD_CONV_DIM), dt.reshape(r, 1, LANES), a_log, d_skip,
      state.reshape(st_shape))
    (x_new,) = _call_whole(_ssd_out_sample_body, "ssd_out_sample",
                           [jax.ShapeDtypeStruct((r, D_MODEL), F32)],
                           x, y.reshape(r, D_INNER), z, norm_w, w_out)
    return x_new, new_conv, new_state.reshape(r, SSD_HEADS, SSD_HEAD_DIM, SSD_STATE)


def _qkv_sample_body(x_ref, nw_ref, wt_ref, cos_ref, sdn_ref, sup_ref, qt_ref, kt_ref, vt_ref, sown_ref):
    rows = x_ref.shape[0]
    tab = lambda r: jnp.broadcast_to(r[...], (ATTN_HEAD_DIM, rows))
    qt, kt, vt = _qkv_t(x_ref[...], nw_ref[...], wt_ref, tab(cos_ref), tab(sdn_ref), tab(sup_ref))
    qt_ref[...] = qt
    kt_ref[...] = kt
    vt_ref[...] = vt
    per_head = (qt * kt).reshape(ATTN_HEADS, ATTN_HEAD_DIM, rows)
    sown_ref[...] = jnp.sum(per_head, axis=1) * ATTN_SCALE


N_PAST_BLK = PAST_LEN // MOBA_BLOCK
PAGES_PER_BLK = MOBA_BLOCK // PAGE_SIZE
HEAD3 = (ATTN_HEADS, ATTN_HEAD_DIM, LANES)


def _moba_sample_body(pt_ref, qt_ref, vnt_ref, sown_ref, *refs):
    del pt_ref
    k_refs, v_refs = refs[:N_PAGES], refs[N_PAGES:2 * N_PAGES]
    ot_ref, s_ref, p_ref = refs[2 * N_PAGES:]
    b = pl.program_id(0)
    blk = MOBA_BLOCK
    is_seq = lax.broadcasted_iota(jnp.int32, (D_MODEL, LANES), 1) == b
    lane = lax.broadcasted_iota(jnp.int32, (ATTN_HEADS, LANES), 1)

    @pl.when(b == 0)
    def _():
        ot_ref[...] = jnp.zeros_like(ot_ref)

    def column(src_ref):
        col = jnp.sum(jnp.where(is_seq, src_ref[...], 0.0), axis=1, keepdims=True)
        return jnp.broadcast_to(col, (D_MODEL, LANES)).reshape(HEAD3)

    qb = column(qt_ref)
    gate = jnp.zeros((ATTN_HEADS, LANES), F32)
    for j in range(N_PAST_BLK):
        raw_sum = jnp.zeros((ATTN_HEADS, 1), F32)
        for w in range(PAGES_PER_BLK):
            page = j * PAGES_PER_BLK + w
            s = jnp.sum(k_refs[page][0] * qb, axis=1)
            s_ref[:, page * PAGE_SIZE:(page + 1) * PAGE_SIZE] = s * ATTN_SCALE
            raw_sum = raw_sum + jnp.sum(s, axis=1, keepdims=True)
        gate = jnp.where(lane == j, raw_sum / blk, gate)

    s_own = jnp.sum(jnp.where(lane == b, sown_ref[...], 0.0), axis=1, keepdims=True)
    m = s_own
    masked = []
    for j in range(N_PAST_BLK):
        sel = _moba_select(gate, j, N_PAST_BLK)
        sb = jnp.where(sel, s_ref[:, j * blk:(j + 1) * blk], NEG_INF)
        masked.append(sb)
        m = jnp.maximum(m, jnp.max(sb, axis=1, keepdims=True))
    p_own = jnp.exp(s_own - m)
    l = p_own
    ps = []
    for sb in masked:
        pb = jnp.exp(sb - m)
        ps.append(pb)
        l = l + jnp.sum(pb, axis=1, keepdims=True)
    for j in range(N_PAST_BLK):
        p_ref[:, j * blk:(j + 1) * blk] = ps[j] / l

    first_lane = lax.broadcasted_iota(jnp.int32, HEAD3, 2) == 0
    w_own = jnp.broadcast_to(p_own / l, (ATTN_HEADS, LANES)).reshape(ATTN_HEADS, 1, LANES)
    acc = jnp.where(first_lane, w_own * column(vnt_ref), 0.0)
    for page in range(N_PAGES):
        p = p_ref[:, page * PAGE_SIZE:(page + 1) * PAGE_SIZE]
        acc = acc + p.reshape(ATTN_HEADS, 1, PAGE_SIZE) * v_refs[page][0]
    o_col = jnp.sum(acc, axis=2, keepdims=True)
    o_full = jnp.broadcast_to(o_col, HEAD3).reshape(D_MODEL, LANES)
    ot_ref[...] = jnp.where(is_seq, o_full, ot_ref[...])


def _moba_sample(qt, vnt, s_own, cache_kt, cache_vt, page_table):
    r = DEC_BATCH
    whole2 = pl.BlockSpec((D_MODEL, r), lambda b, pt: (0, 0))
    page = lambda w: pl.BlockSpec((1,) + HEAD3, lambda b, pt: (pt[b * N_PAGES + w], 0, 0, 0))
    pages = [page(w) for w in range(N_PAGES)]
    grid_spec = pltpu.PrefetchScalarGridSpec(
        num_scalar_prefetch=1,
        grid=(r,),
        in_specs=[whole2, whole2, pl.BlockSpec((ATTN_HEADS, r), lambda b, pt: (0, 0))] + pages + pages,
        out_specs=whole2,
        scratch_shapes=[
            pltpu.VMEM((ATTN_HEADS, PAST_LEN), F32),
            pltpu.VMEM((ATTN_HEADS, PAST_LEN), F32),
        ],
    )
    return pl.pallas_call(
        _moba_sample_body,
        grid_spec=grid_spec,
        out_shape=jax.ShapeDtypeStruct((D_MODEL, r), F32),
        compiler_params=_params("arbitrary"),
        name="moba_sample",
    )(page_table.reshape(-1), qt, vnt, s_own, *([cache_kt] * N_PAGES), *([cache_vt] * N_PAGES))


def _rope_tables(pos):
    half = ROT_DIM // 2
    inv_freq = ROPE_THETA ** (-(jnp.arange(half, dtype=F32) * 2.0) / ROT_DIM)
    ang = pos.astype(F32)[:, None] * inv_freq
    cos, sin = jnp.cos(ang), jnp.sin(ang)
    ones = jnp.ones((pos.shape[0], ATTN_HEAD_DIM - ROT_DIM), F32)
    zeros = jnp.zeros((pos.shape[0], ATTN_HEAD_DIM - ROT_DIM), F32)
    zh = jnp.zeros_like(sin)
    return (jnp.concatenate([cos, cos, ones], axis=1),
            jnp.concatenate([-sin, zh, zeros], axis=1),
            jnp.concatenate([zh, sin, zeros], axis=1))


def _pad_lanes(a):
    return jnp.pad(a, [(0, 0)] * (a.ndim - 1) + [(0, LANES - a.shape[-1])])


def kernel(x_prompt, x_sample, state_sconv, state_ssm, state_ssm_conv, cache_k, cache_v, state_ffn_conv, page_table, norm_mix_w, norm_ffn_w, norm_final_w, sconv_w_in, sconv_conv_w, sconv_w_out, ssd_w_in, ssd_conv_w, ssd_conv_b, ssd_dt_bias, ssd_a_log, ssd_d, ssd_norm_w, ssd_w_out, attn_w_qkv, attn_w_o, ffn_w_up, ffn_conv_w, ffn_conv_b, ffn_w_down):
    xp = x_prompt.reshape(N_TOK, D_MODEL)
    xs = x_sample.reshape(DEC_BATCH, D_MODEL)
    fnw = norm_final_w.reshape(1, D_MODEL)
    sconv_p, sconv_s, ffnc_p, ffnc_s = [], [], [], []
    for i in range(DEPTH):
        kind, j = i % N_MIXERS, i // N_MIXERS
        nw = norm_mix_w[i].reshape(1, D_MODEL)
        if kind == 0:
            w_in, w_out = sconv_w_in[j].astype(BF16), sconv_w_out[j].astype(BF16)
            xp, st = _sconv_prompt(xp, nw, w_in, sconv_conv_w[j], w_out)
            xs, st_s = _sconv_sample(xs, nw, w_in, sconv_conv_w[j], w_out, state_sconv[j])
            sconv_p.append(st)
            sconv_s.append(st_s)
        elif kind == 1:
            w_in = ssd_w_in[j]
            w_z = w_in[:, :D_INNER].astype(BF16)
            w_x = w_in[:, D_INNER:D_INNER + SSD_CONV_DIM].astype(BF16)
            w_dt = _pad_lanes(w_in[:, D_INNER + SSD_CONV_DIM:]).astype(BF16)
            w_out = ssd_w_out[j].astype(BF16)
            conv_b = ssd_conv_b[j].reshape(1, SSD_CONV_DIM)
            dt_bias = _pad_lanes(ssd_dt_bias[j].reshape(1, SSD_HEADS))
            a_log = _pad_lanes(ssd_a_log[j].reshape(1, SSD_HEADS))
            d_skip = jnp.repeat(ssd_d[j], SSD_HEAD_DIM).reshape(1, D_INNER)
            norm_w = ssd_norm_w[j].reshape(1, D_INNER)
            z, xbc, dt, ssmc_p = _ssd_inproj_prompt(xp, nw, w_z, w_x, w_dt, ssd_conv_w[j], conv_b, dt_bias)
            d_skip_rows = jnp.broadcast_to(d_skip.reshape(D_INNER, 1), (D_INNER, LANES))
            xp, ssm_p = _ssd_scan_prompt(xp, z, xbc, dt, a_log, d_skip_rows, norm_w, w_out)
            ssm_p = ssm_p.reshape(BATCH, SSD_HEADS, SSD_HEAD_DIM, SSD_STATE)
            xs, ssmc_s, ssm_s = _ssd_sample(xs, nw, w_z, w_x, w_dt, ssd_conv_w[j], conv_b, dt_bias,
                                            a_log, d_skip, norm_w, w_out, state_ssm[j], state_ssm_conv[j])
        else:
            w_qkv, w_o = attn_w_qkv[j].astype(BF16), attn_w_o[j].astype(BF16)
            w_qkv_t = w_qkv.T
            tabs = [t.T for t in _rope_tables(jnp.arange(SEQ))]
            qt, kt_p, vt_p, k_rows = _qkv_prompt(xp, nw, w_qkv_t, *tabs)
            xp = _proj_residual_t(_moba_prompt(qt, k_rows, vt_p), w_o, xp, PROJ_TL)
            tabs = [t.T for t in _rope_tables(jnp.full((1,), PAST_LEN))]
            sds = jax.ShapeDtypeStruct((D_MODEL, DEC_BATCH), F32)
            own = jax.ShapeDtypeStruct((ATTN_HEADS, DEC_BATCH), F32)
            qt_s, kt_s, vt_s, s_own = _call_whole(_qkv_sample_body, "qkv_sample", [sds, sds, sds, own],
                                                  xs, nw, w_qkv_t, *tabs)
            pages = lambda c: jnp.transpose(c, (0, 2, 3, 1))
            ot_s = _moba_sample(qt_s, vt_s, s_own, pages(cache_k[j]), pages(cache_v[j]), page_table)
            xs = _proj_residual_t(ot_s[None], w_o, xs, DEC_BATCH)
        nwf = norm_ffn_w[i].reshape(1, D_MODEL)
        w_up, w_down = ffn_w_up[i].astype(BF16), ffn_w_down[i].astype(BF16)
        conv_b = ffn_conv_b[i].reshape(1, 2 * D_FF)
        final = i == DEPTH - 1
        xp, fc_p = _ffn_prompt(xp, nwf, w_up, ffn_conv_w[i], conv_b, w_down, fnw, final)
        xs, fc_s = _ffn_sample(xs, nwf, w_up, ffn_conv_w[i], conv_b, w_down, fnw, state_ffn_conv[i], final)
        ffnc_p.append(fc_p)
        ffnc_s.append(fc_s)
    heads = (ATTN_HEADS, ATTN_HEAD_DIM)
    kv_p = lambda t: jnp.transpose(t.reshape((BATCH,) + heads + (SEQ,)), (0, 3, 1, 2))[None]
    kv_s = lambda t: jnp.transpose(t.reshape(heads + (DEC_BATCH,)), (2, 0, 1))[None, :, None]
    return (xp.reshape(BATCH, SEQ, D_MODEL), xs.reshape(DEC_BATCH, 1, D_MODEL),
            jnp.stack(sconv_p), jnp.stack(sconv_s),
            ssm_p[None], ssm_s[None], ssmc_p[None], ssmc_s[None],
            kv_p(kt_p), kv_p(vt_p), kv_s(kt_s), kv_s(vt_s),
            jnp.stack(ffnc_p), jnp.stack(ffnc_s))
```

```python
import functools

import jax
import jax.numpy as jnp
from jax import lax
from jax.experimental import pallas as pl
from jax.experimental.pallas import tpu as pltpu

F32 = jnp.float32
BF16 = jnp.bfloat16

D_MODEL = 1024
BATCH = 8
SEQ = 2048
DEPTH = 4
DEC_BATCH = 128
PAST_LEN = 2048
PAGE_SIZE = 128
N_PAGES = PAST_LEN // PAGE_SIZE
N_MIXERS = 3
RMS_EPS = 1e-6
SCONV_WIDTH = 3
D_INNER = 2 * D_MODEL
SSD_HEAD_DIM = 64
SSD_HEADS = D_INNER // SSD_HEAD_DIM
SSD_GROUPS = 4
SSD_STATE = 128
SSD_CONV_WIDTH = 4
SSD_CONV_DIM = D_INNER + 2 * SSD_GROUPS * SSD_STATE
SSD_CHUNK = 128
SSD_PAIRS = SSD_HEADS // 2
ATTN_HEAD_DIM = 64
ATTN_HEADS = D_MODEL // ATTN_HEAD_DIM
MOBA_BLOCK = 256
MOBA_TOP_K = 3
ROPE_THETA = 500000.0
ROT_DIM = ATTN_HEAD_DIM // 4
D_FF = 2816
FFN_CONV_WIDTH = 3

LANES = 128
SUBLANES = 8
VMEM_LIMIT = 56 * 1024 * 1024
N_TOK = BATCH * SEQ
NEG_INF = float("-inf")


def _params(*sem):
    return pltpu.CompilerParams(dimension_semantics=sem, vmem_limit_bytes=VMEM_LIMIT)


def _bdot(a, b):
    return jnp.dot(a.astype(BF16), b.astype(BF16), preferred_element_type=F32)


def _bdot_nt(a, b):
    return lax.dot_general(a.astype(BF16), b.astype(BF16), (((1,), (1,)), ((), ())),
                           preferred_element_type=F32)


def _rms(x, w):
    inv = lax.rsqrt(jnp.mean(x * x, axis=-1, keepdims=True) + RMS_EPS)
    return (x * inv) * w


def _silu(x):
    return x * (1.0 / (1.0 + jnp.exp(-x)))


def _softplus(x):
    return jnp.maximum(x, 0.0) + jnp.log1p(jnp.exp(-jnp.abs(x)))


def _conv_rows(u, w, prev):
    width = w.shape[0]
    row8 = lax.broadcasted_iota(jnp.int32, (SUBLANES, u.shape[1]), 0)
    out = u * w[width - 1:width, :]
    for k in range(1, width):
        sh = pltpu.roll(u, k, axis=0)
        head = sh[:SUBLANES]
        for t in range(k):
            src = SUBLANES - k + t
            head = jnp.where(row8 == t, prev[src:src + 1, :], head)
        sh = jnp.concatenate([head, sh[SUBLANES:]], axis=0)
        out = out + sh * w[width - 1 - k:width - k, :]
    return out


def _conv_step(u, w, past):
    width = w.shape[0]
    out = u * w[width - 1:width, :]
    for k in range(width - 1):
        out = out + past[k] * w[k:k + 1, :]
    return out


SCONV_TL = 1024


def _sconv_prompt_body(x_ref, nw_ref, win_ref, cw_ref, wout_ref, o_ref, st_ref, carry_ref):
    l = pl.program_id(1)

    @pl.when(l == 0)
    def _():
        carry_ref[...] = jnp.zeros_like(carry_ref)

    x = x_ref[...]
    h = _rms(x, nw_ref[...]).astype(BF16)
    p = jnp.dot(h, win_ref[...], preferred_element_type=F32)
    g = p[:, D_MODEL:2 * D_MODEL] * p[:, 2 * D_MODEL:]
    c = _conv_rows(g, cw_ref[...], carry_ref[...])
    carry_ref[...] = g[SCONV_TL - SUBLANES:, :]
    o_ref[...] = x + _bdot(p[:, :D_MODEL] * c, wout_ref[...])

    @pl.when(l == pl.num_programs(1) - 1)
    def _():
        st_ref[0] = g[SCONV_TL - (SCONV_WIDTH - 1):, :]


def _sconv_prompt(x, nw, w_in, conv_w, w_out):
    nl = SEQ // SCONV_TL
    row = lambda b, l: (b * nl + l, 0)
    const = lambda b, l: (0, 0)
    return pl.pallas_call(
        _sconv_prompt_body,
        grid=(BATCH, nl),
        in_specs=[
            pl.BlockSpec((SCONV_TL, D_MODEL), row),
            pl.BlockSpec((1, D_MODEL), const),
            pl.BlockSpec((D_MODEL, 3 * D_MODEL), const),
            pl.BlockSpec((SCONV_WIDTH, D_MODEL), const),
            pl.BlockSpec((D_MODEL, D_MODEL), const),
        ],
        out_specs=[
            pl.BlockSpec((SCONV_TL, D_MODEL), row),
            pl.BlockSpec((1, SCONV_WIDTH - 1, D_MODEL), lambda b, l: (b, 0, 0)),
        ],
        out_shape=[
            jax.ShapeDtypeStruct((N_TOK, D_MODEL), F32),
            jax.ShapeDtypeStruct((BATCH, SCONV_WIDTH - 1, D_MODEL), F32),
        ],
        scratch_shapes=[pltpu.VMEM((SUBLANES, D_MODEL), F32)],
        compiler_params=_params("arbitrary", "arbitrary"),
        name="sconv_prompt",
    )(x, nw, w_in, conv_w, w_out)


FFN_TL = 1024
FFN_CHUNK = D_FF // 2


def _ffn_prompt_body(x_ref, nw_ref, wup_ref, cw_ref, cb_ref, wdn_ref, fnw_ref, o_ref, st_ref,
                     carry_ref, *, final):
    l = pl.program_id(1)

    @pl.when(l == 0)
    def _():
        carry_ref[...] = jnp.zeros_like(carry_ref)

    x = x_ref[...]
    h = _rms(x, nw_ref[...]).astype(BF16)
    acc = x
    last = l == pl.num_programs(1) - 1
    for c in range(D_FF // FFN_CHUNK):
        halves = []
        for base in (0, D_FF):
            lo = base + c * FFN_CHUNK
            hi = lo + FFN_CHUNK
            u = jnp.dot(h, wup_ref[:, lo:hi], preferred_element_type=F32)
            conv = _conv_rows(u, cw_ref[:, lo:hi], carry_ref[:, lo:hi]) + cb_ref[:, lo:hi]
            carry_ref[:, lo:hi] = u[FFN_TL - SUBLANES:, :]

            @pl.when(last)
            def _(u=u, lo=lo, hi=hi):
                st_ref[0, :, lo:hi] = u[FFN_TL - (FFN_CONV_WIDTH - 1):, :]

            halves.append(conv)
        a, g = halves
        acc = acc + _bdot(_silu(g) * a, wdn_ref[c * FFN_CHUNK:(c + 1) * FFN_CHUNK, :])
    if final:
        acc = _rms(acc, fnw_ref[...])
    o_ref[...] = acc


def _ffn_prompt(x, nw, w_up, conv_w, conv_b, w_down, fnw, final):
    nl = SEQ // FFN_TL
    row = lambda b, l: (b * nl + l, 0)
    const = lambda b, l: (0, 0)
    once = pl.Buffered(1)
    return pl.pallas_call(
        functools.partial(_ffn_prompt_body, final=final),
        grid=(BATCH, nl),
        in_specs=[
            pl.BlockSpec((FFN_TL, D_MODEL), row),
            pl.BlockSpec((1, D_MODEL), const),
            pl.BlockSpec((D_MODEL, 2 * D_FF), const, pipeline_mode=once),
            pl.BlockSpec((FFN_CONV_WIDTH, 2 * D_FF), const),
            pl.BlockSpec((1, 2 * D_FF), const),
            pl.BlockSpec((D_FF, D_MODEL), const, pipeline_mode=once),
            pl.BlockSpec((1, D_MODEL), const),
        ],
        out_specs=[
            pl.BlockSpec((FFN_TL, D_MODEL), row),
            pl.BlockSpec((1, FFN_CONV_WIDTH - 1, 2 * D_FF), lambda b, l: (b, 0, 0)),
        ],
        out_shape=[
            jax.ShapeDtypeStruct((N_TOK, D_MODEL), F32),
            jax.ShapeDtypeStruct((BATCH, FFN_CONV_WIDTH - 1, 2 * D_FF), F32),
        ],
        scratch_shapes=[pltpu.VMEM((SUBLANES, 2 * D_FF), F32)],
        compiler_params=_params("arbitrary", "arbitrary"),
        name="ffn_prompt",
    )(x, nw, w_up, conv_w, conv_b, w_down, fnw)


SSD_TL = 256


def _ssd_inproj_prompt_body(x_ref, nw_ref, wz_ref, wx_ref, wdt_ref, cw_ref, cb_ref, dtb_ref,
                            z_ref, xbc_ref, dt_ref, st_ref, carry_ref):
    l = pl.program_id(1)

    @pl.when(l == 0)
    def _():
        carry_ref[...] = jnp.zeros_like(carry_ref)

    h = _rms(x_ref[...], nw_ref[...]).astype(BF16)
    z_ref[...] = jnp.dot(h, wz_ref[...], preferred_element_type=F32)
    dt_ref[...] = _softplus(jnp.dot(h, wdt_ref[...], preferred_element_type=F32) + dtb_ref[...])
    u = jnp.dot(h, wx_ref[...], preferred_element_type=F32)
    xbc_ref[...] = _silu(_conv_rows(u, cw_ref[...], carry_ref[...]) + cb_ref[...])
    carry_ref[...] = u[SSD_TL - SUBLANES:, :]

    @pl.when(l == pl.num_programs(1) - 1)
    def _():
        st_ref[0] = u[SSD_TL - (SSD_CONV_WIDTH - 1):, :]


def _ssd_inproj_prompt(x, nw, w_z, w_x, w_dt, conv_w, conv_b, dt_bias):
    nl = SEQ // SSD_TL
    row = lambda b, l: (b * nl + l, 0)
    const = lambda b, l: (0, 0)
    return pl.pallas_call(
        _ssd_inproj_prompt_body,
        grid=(BATCH, nl),
        in_specs=[
            pl.BlockSpec((SSD_TL, D_MODEL), row),
            pl.BlockSpec((1, D_MODEL), const),
            pl.BlockSpec((D_MODEL, D_INNER), const),
            pl.BlockSpec((D_MODEL, SSD_CONV_DIM), const),
            pl.BlockSpec((D_MODEL, LANES), const),
            pl.BlockSpec((SSD_CONV_WIDTH, SSD_CONV_DIM), const),
            pl.BlockSpec((1, SSD_CONV_DIM), const),
            pl.BlockSpec((1, LANES), const),
        ],
        out_specs=[
            pl.BlockSpec((SSD_TL, D_INNER), row),
            pl.BlockSpec((SSD_TL, SSD_CONV_DIM), row),
            pl.BlockSpec((SSD_TL, LANES), row),
            pl.BlockSpec((1, SSD_CONV_WIDTH - 1, SSD_CONV_DIM), lambda b, l: (b, 0, 0)),
        ],
        out_shape=[
            jax.ShapeDtypeStruct((N_TOK, D_INNER), F32),
            jax.ShapeDtypeStruct((N_TOK, SSD_CONV_DIM), F32),
            jax.ShapeDtypeStruct((N_TOK, LANES), F32),
            jax.ShapeDtypeStruct((BATCH, SSD_CONV_WIDTH - 1, SSD_CONV_DIM), F32),
        ],
        scratch_shapes=[pltpu.VMEM((SUBLANES, SSD_CONV_DIM), F32)],
        compiler_params=_params("arbitrary", "arbitrary"),
        name="ssd_inproj_prompt",
    )(x, nw, w_z, w_x, w_dt, conv_w, conv_b, dt_bias)


def _cumsum_rows(x):
    n = x.shape[0]
    row = lax.broadcasted_iota(jnp.int32, x.shape, 0)
    k = 1
    while k < n:
        x = x + jnp.where(row >= k, pltpu.roll(x, k, axis=0), 0.0)
        k *= 2
    return x


def _pair_lanes(vals, h0, shape):
    lane = lax.broadcasted_iota(jnp.int32, shape, 1)
    return jnp.where(lane < SSD_HEAD_DIM, vals[:, h0:h0 + 1], vals[:, h0 + 1:h0 + 2])


def _group_norm_gate(y, z, nw):
    y = y * _silu(z)
    gw = D_INNER // SSD_GROUPS
    outs = []
    for g in range(SSD_GROUPS):
        yg = y[:, g * gw:(g + 1) * gw]
        outs.append(_rms(yg, nw[:, g * gw:(g + 1) * gw]))
    return jnp.concatenate(outs, axis=1)


def _ssd_scan_prompt_body(x_ref, z_ref, xbc_ref, dt_ref, alog_ref, dskip_ref, nw_ref, wout_ref,
                          o_ref, st_ref, y_ref):
    c = pl.program_id(1)

    @pl.when(c == 0)
    def _():
        st_ref[...] = jnp.zeros_like(st_ref)

    q = SSD_CHUNK
    p2 = 2 * SSD_HEAD_DIM
    dt = dt_ref[...]
    acum = _cumsum_rows(dt * (-jnp.exp(alog_ref[...])))
    last = acum[q - 1:q, :]
    acum_t = acum.T
    dt_t = dt.T
    w_t = (dt * jnp.exp(last - acum)).T
    e_acum_t = jnp.exp(acum_t)
    e_chunk = jnp.exp(last)
    row = lax.broadcasted_iota(jnp.int32, (q, q), 0)
    col = lax.broadcasted_iota(jnp.int32, (q, q), 1)
    causal_t = col >= row
    row_lo = lax.broadcasted_iota(jnp.int32, (p2, q), 0) < SSD_HEAD_DIM

    def head_rows(vals_t, h0):
        return jnp.where(row_lo, vals_t[h0:h0 + 1, :], vals_t[h0 + 1:h0 + 2, :])

    for g in range(SSD_GROUPS):
        b_g = xbc_ref[:, D_INNER + g * SSD_STATE:D_INNER + (g + 1) * SSD_STATE].astype(BF16)
        c_g = xbc_ref[:, D_INNER + (SSD_GROUPS + g) * SSD_STATE:D_INNER + (SSD_GROUPS + g + 1) * SSD_STATE]
        cb_t = _bdot_nt(b_g, c_g)
        c_t = c_g.T.astype(BF16)
        for j in range(g * SSD_PAIRS // SSD_GROUPS, (g + 1) * SSD_PAIRS // SSD_GROUPS):
            h0 = 2 * j
            xs_t = xbc_ref[:, j * LANES:(j + 1) * LANES].T
            xdt_t = (xs_t * head_rows(dt_t, h0)).astype(BF16)
            yd = []
            for hh in range(2):
                h = h0 + hh
                seg = acum_t[h:h + 1, :] - acum[:, h:h + 1]
                decay = jnp.exp(jnp.where(causal_t, seg, NEG_INF))
                yd.append(_bdot(xdt_t[hh * SSD_HEAD_DIM:(hh + 1) * SSD_HEAD_DIM, :], cb_t * decay))
            state = st_ref[0, j]
            y_t = jnp.concatenate(yd, axis=0) + _bdot(state, c_t) * head_rows(e_acum_t, h0)
            y_t = y_t + xs_t * dskip_ref[j * p2:(j + 1) * p2, :]
            y_ref[:, j * LANES:(j + 1) * LANES] = y_t.T
            upd = _bdot(xs_t * head_rows(w_t, h0), b_g)
            scale = jnp.where(row_lo, e_chunk[:, h0:h0 + 1], e_chunk[:, h0 + 1:h0 + 2])
            st_ref[0, j] = state * scale + upd

    yn = _group_norm_gate(y_ref[...], z_ref[...], nw_ref[...])
    o_ref[...] = x_ref[...] + _bdot(yn, wout_ref[...])


def _ssd_scan_prompt(x, z, xbc, dt, a_log, d_skip, nw, w_out):
    nc = SEQ // SSD_CHUNK
    row = lambda b, c: (b * nc + c, 0)
    const = lambda b, c: (0, 0)
    return pl.pallas_call(
        _ssd_scan_prompt_body,
        grid=(BATCH, nc),
        in_specs=[
            pl.BlockSpec((SSD_CHUNK, D_MODEL), row),
            pl.BlockSpec((SSD_CHUNK, D_INNER), row),
            pl.BlockSpec((SSD_CHUNK, SSD_CONV_DIM), row),
            pl.BlockSpec((SSD_CHUNK, LANES), row),
            pl.BlockSpec((1, LANES), const),
            pl.BlockSpec((D_INNER, LANES), const),
            pl.BlockSpec((1, D_INNER), const),
            pl.BlockSpec((D_INNER, D_MODEL), const),
        ],
        out_specs=[
            pl.BlockSpec((SSD_CHUNK, D_MODEL), row),
            pl.BlockSpec((1, SSD_PAIRS, 2 * SSD_HEAD_DIM, SSD_STATE), lambda b, c: (b, 0, 0, 0)),
        ],
        out_shape=[
            jax.ShapeDtypeStruct((N_TOK, D_MODEL), F32),
            jax.ShapeDtypeStruct((BATCH, SSD_PAIRS, 2 * SSD_HEAD_DIM, SSD_STATE), F32),
        ],
        scratch_shapes=[pltpu.VMEM((SSD_CHUNK, D_INNER), F32)],
        compiler_params=_params("arbitrary", "arbitrary"),
        name="ssd_scan_prompt",
    )(x, z, xbc, dt, a_log, d_skip, nw, w_out)


QKV_TL = 512


def _rope_t(t, cos, sin_dn, sin_up):
    reps = t.shape[0] // ATTN_HEAD_DIM
    half = ROT_DIM // 2
    tile = lambda a: jnp.concatenate([a] * reps, axis=0)
    return (t * tile(cos)
            + pltpu.roll(t, t.shape[0] - half, axis=0) * tile(sin_dn)
            + pltpu.roll(t, half, axis=0) * tile(sin_up))


def _qkv_t(x, nw, wt_ref, cos, sdn, sup):
    ht = _rms(x, nw).T.astype(BF16)
    qt = _rope_t(jnp.dot(wt_ref[:D_MODEL, :], ht, preferred_element_type=F32), cos, sdn, sup)
    kt = _rope_t(jnp.dot(wt_ref[D_MODEL:2 * D_MODEL, :], ht, preferred_element_type=F32), cos, sdn, sup)
    vt = jnp.dot(wt_ref[2 * D_MODEL:, :], ht, preferred_element_type=F32)
    return qt, kt, vt


def _qkv_prompt_body(x_ref, nw_ref, wt_ref, cos_ref, sdn_ref, sup_ref, qt_ref, kt_ref, vt_ref, k_ref):
    qt, kt, vt = _qkv_t(x_ref[...], nw_ref[...], wt_ref, cos_ref[...], sdn_ref[...], sup_ref[...])
    qt_ref[0] = qt
    kt_ref[0] = kt
    vt_ref[0] = vt
    k_ref[...] = kt.T


def _qkv_prompt(x, nw, w_qkv_t, cos, sdn, sup):
    nl = SEQ // QKV_TL
    row = lambda b, l: (b * nl + l, 0)
    const = lambda b, l: (0, 0)
    tab = pl.BlockSpec((ATTN_HEAD_DIM, QKV_TL), lambda b, l: (0, l))
    tspec = pl.BlockSpec((1, D_MODEL, QKV_TL), lambda b, l: (b, 0, l))
    tout = jax.ShapeDtypeStruct((BATCH, D_MODEL, SEQ), F32)
    return pl.pallas_call(
        _qkv_prompt_body,
        grid=(BATCH, nl),
        in_specs=[
            pl.BlockSpec((QKV_TL, D_MODEL), row),
            pl.BlockSpec((1, D_MODEL), const),
            pl.BlockSpec((3 * D_MODEL, D_MODEL), const),
            tab, tab, tab,
        ],
        out_specs=[tspec, tspec, tspec, pl.BlockSpec((QKV_TL, D_MODEL), row)],
        out_shape=[tout, tout, tout, jax.ShapeDtypeStruct((N_TOK, D_MODEL), F32)],
        compiler_params=_params("arbitrary", "arbitrary"),
        name="qkv_prompt",
    )(x, nw, w_qkv_t, cos, sdn, sup)


N_BLK = SEQ // MOBA_BLOCK
ATTN_SCALE = ATTN_HEAD_DIM ** -0.5
LOG2_E = 1.4426950408889634


def _moba_select(gate, n, own):
    ax = gate.ndim - 1
    lane = lax.broadcasted_iota(jnp.int32, gate.shape, ax)
    g_n = jnp.sum(jnp.where(lane == n, gate, 0.0), axis=ax, keepdims=True)
    ahead = (gate > g_n) | ((gate == g_n) & (lane < n))
    rank = jnp.sum(jnp.where(ahead & (lane < own), 1.0, 0.0), axis=ax, keepdims=True)
    return rank < MOBA_TOP_K


def _moba_bias_rows(gate_t, own):
    ax = gate_t.ndim - 2
    row = lax.broadcasted_iota(jnp.int32, gate_t.shape, ax)
    past = row < own
    bias = jnp.full(gate_t.shape, NEG_INF, F32)
    for n in range(gate_t.shape[ax] - 1):
        g_n = gate_t[:, n:n + 1, :]
        ahead = (gate_t > g_n) | ((gate_t == g_n) & (row < n))
        rank = jnp.sum(jnp.where(ahead & past, 1.0, 0.0), axis=ax, keepdims=True)
        chosen = (rank < MOBA_TOP_K) & (row == n) & past
        bias = jnp.where(chosen, 0.0, bias)
    return bias


def _dot3(a, b):
    a_hi = a.astype(BF16)
    b_hi = b.astype(BF16)
    a_lo = (a - a_hi.astype(F32)).astype(BF16)
    b_lo = (b - b_hi.astype(F32)).astype(BF16)
    return (jnp.dot(a_hi, b_hi, preferred_element_type=F32)
            + jnp.dot(a_hi, b_lo, preferred_element_type=F32)
            + jnp.dot(a_lo, b_hi, preferred_element_type=F32))


MOBA_HEADS = 4
MOBA_ROWS = MOBA_HEADS * ATTN_HEAD_DIM


def _moba_prompt_body(qt_ref, k_ref, vt_ref, ot_ref, kmean_ref, bias_ref, sa_ref, sb_ref):
    i = pl.program_id(2)
    blk = MOBA_BLOCK
    dh = ATTN_HEAD_DIM
    heads = range(MOBA_HEADS)

    @pl.when(i == 0)
    def _():
        for n in range(N_BLK):
            kmean_ref[n:n + 1, :] = jnp.mean(k_ref[n * blk:(n + 1) * blk, :], axis=0, keepdims=True)

    qt = qt_ref[0]
    q_row = lax.broadcasted_iota(jnp.int32, qt.shape, 0)
    km = kmean_ref[...]
    km_lane = lax.broadcasted_iota(jnp.int32, km.shape, 1)
    qs = qt * (ATTN_SCALE * LOG2_E)
    q_w = [jnp.where((q_row >= hh * dh) & (q_row < (hh + 1) * dh), qs, 0.0).astype(BF16) for hh in heads]
    km_heads = [jnp.where((km_lane >= hh * dh) & (km_lane < (hh + 1) * dh), km, 0.0) for hh in heads]
    gates = _dot3(jnp.concatenate(km_heads, axis=0), qt)
    bias_ref[...] = _moba_bias_rows(gates.reshape(MOBA_HEADS, N_BLK, blk), i)

    key = lax.broadcasted_iota(jnp.int32, (blk, blk), 0)
    qry = lax.broadcasted_iota(jnp.int32, (blk, blk), 1)

    def scores(n, dst_ref):
        k_n = k_ref[pl.ds(pl.multiple_of(n * blk, blk), blk), :].astype(BF16)
        for hh in heads:
            dst_ref[hh] = jnp.dot(k_n, q_w[hh], preferred_element_type=F32)

    def absorb(n, masked, state):
        vt_n = vt_ref[0, :, pl.ds(pl.multiple_of(n * blk, blk), blk)].astype(BF16)
        out = []
        for hh in heads:
            m, l, acc = state[3 * hh:3 * hh + 3]
            s = masked(hh)
            m_new = jnp.maximum(m, jnp.max(s, axis=0, keepdims=True))
            m_ref = jnp.where(m_new == NEG_INF, 0.0, m_new)
            alpha = jnp.exp2(m - m_ref)
            p = jnp.exp2(s - m_ref)
            l = l * alpha + jnp.sum(p, axis=0, keepdims=True)
            acc = acc * alpha + jnp.dot(vt_n[hh * dh:(hh + 1) * dh, :], p.astype(BF16),
                                        preferred_element_type=F32)
            out += [m_new, l, acc]
        return tuple(out)

    def past(n, src_ref):
        return lambda hh: src_ref[hh] + bias_ref[hh, pl.ds(n, 1), :]

    def block_pair(p, state):
        scores(jnp.minimum(2 * p + 1, i), sb_ref)
        state = absorb(2 * p, past(2 * p, sa_ref), state)
        scores(jnp.minimum(2 * p + 2, i), sa_ref)
        return absorb(jnp.minimum(2 * p + 1, i), past(2 * p + 1, sb_ref), state)

    scores(0, sa_ref)
    init = (jnp.full((1, blk), NEG_INF, F32), jnp.zeros((1, blk), F32), jnp.zeros((dh, blk), F32))
    state = lax.fori_loop(0, (i + 1) // 2, block_pair, init * MOBA_HEADS)
    final = absorb(i, lambda hh: jnp.where(key <= qry, sa_ref[hh], NEG_INF), state)
    ot_ref[0] = jnp.concatenate([final[3 * hh + 2] / final[3 * hh + 1] for hh in heads], axis=0)


def _moba_prompt(qt, k, vt):
    groups = ATTN_HEADS // MOBA_HEADS
    qspec = pl.BlockSpec((1, MOBA_ROWS, MOBA_BLOCK), lambda b, p, i: (b, p, i))
    return pl.pallas_call(
        _moba_prompt_body,
        grid=(BATCH, groups, N_BLK),
        in_specs=[
            qspec,
            pl.BlockSpec((SEQ, MOBA_ROWS), lambda b, p, i: (b, p)),
            pl.BlockSpec((1, MOBA_ROWS, SEQ), lambda b, p, i: (b, p, 0)),
        ],
        out_specs=qspec,
        out_shape=jax.ShapeDtypeStruct((BATCH, D_MODEL, SEQ), F32),
        scratch_shapes=[pltpu.VMEM((N_BLK, MOBA_ROWS), F32),
                        pltpu.VMEM((MOBA_HEADS, N_BLK, MOBA_BLOCK), F32),
                        pltpu.VMEM((MOBA_HEADS, MOBA_BLOCK, MOBA_BLOCK), F32),
                        pltpu.VMEM((MOBA_HEADS, MOBA_BLOCK, MOBA_BLOCK), F32)],
        compiler_params=_params("arbitrary", "arbitrary", "arbitrary"),
        name="moba_prompt",
    )(qt, k, vt)


PROJ_TL = 512


def _proj_residual_t_body(at_ref, w_ref, x_ref, o_ref):
    o_ref[...] = x_ref[...] + _bdot(at_ref[0].T, w_ref[...])


def _proj_residual_t(at, w, x, tl):
    groups, kdim, per = at.shape
    nl = per // tl
    row = lambda g, l: (g * nl + l, 0)
    return pl.pallas_call(
        _proj_residual_t_body,
        grid=(groups, nl),
        in_specs=[
            pl.BlockSpec((1, kdim, tl), lambda g, l: (g, 0, l)),
            pl.BlockSpec(w.shape, lambda g, l: (0, 0)),
            pl.BlockSpec((tl, w.shape[1]), row),
        ],
        out_specs=pl.BlockSpec((tl, w.shape[1]), row),
        out_shape=jax.ShapeDtypeStruct(x.shape, F32),
        compiler_params=_params("arbitrary", "arbitrary"),
        name="proj_residual",
    )(at, w, x)


def _whole(shape):
    return pl.BlockSpec(shape, lambda: (0,) * len(shape))


def _call_whole(body, name, out_shapes, *args):
    return pl.pallas_call(
        body,
        in_specs=[_whole(a.shape) for a in args],
        out_specs=[_whole(s.shape) for s in out_shapes],
        out_shape=out_shapes,
        compiler_params=pltpu.CompilerParams(vmem_limit_bytes=VMEM_LIMIT),
        name=name,
    )(*args)


def _sconv_sample_body(x_ref, nw_ref, win_ref, cw_ref, wout_ref, p0_ref, p1_ref, o_ref, g_ref):
    x = x_ref[...]
    h = _rms(x, nw_ref[...]).astype(BF16)
    p = jnp.dot(h, win_ref[...], preferred_element_type=F32)
    g = p[:, D_MODEL:2 * D_MODEL] * p[:, 2 * D_MODEL:]
    c = _conv_step(g, cw_ref[...], [p0_ref[...], p1_ref[...]])
    o_ref[...] = x + _bdot(p[:, :D_MODEL] * c, wout_ref[...])
    g_ref[...] = g


def _sconv_sample(x, nw, w_in, conv_w, w_out, past):
    sds = jax.ShapeDtypeStruct((DEC_BATCH, D_MODEL), F32)
    x_new, g = _call_whole(_sconv_sample_body, "sconv_sample", [sds, sds],
                           x, nw, w_in, conv_w, w_out, past[:, 0], past[:, 1])
    return x_new, jnp.stack([past[:, 1], g], axis=1)


def _ffn_sample_body(x_ref, nw_ref, wup_ref, cw_ref, cb_ref, wdn_ref, fnw_ref, p0_ref, p1_ref,
                     o_ref, u_ref, *, final):
    x = x_ref[...]
    h = _rms(x, nw_ref[...]).astype(BF16)
    acc = x
    for c in range(D_FF // FFN_CHUNK):
        halves = []
        for base in (0, D_FF):
            lo = base + c * FFN_CHUNK
            hi = lo + FFN_CHUNK
            u = jnp.dot(h, wup_ref[:, lo:hi], preferred_element_type=F32)
            u_ref[:, lo:hi] = u
            halves.append(_conv_step(u, cw_ref[:, lo:hi], [p0_ref[:, lo:hi], p1_ref[:, lo:hi]])
                          + cb_ref[:, lo:hi])
        a, g = halves
        acc = acc + _bdot(_silu(g) * a, wdn_ref[c * FFN_CHUNK:(c + 1) * FFN_CHUNK, :])
    if final:
        acc = _rms(acc, fnw_ref[...])
    o_ref[...] = acc


def _ffn_sample(x, nw, w_up, conv_w, conv_b, w_down, fnw, past, final):
    outs = [jax.ShapeDtypeStruct((DEC_BATCH, D_MODEL), F32),
            jax.ShapeDtypeStruct((DEC_BATCH, 2 * D_FF), F32)]
    x_new, u = _call_whole(functools.partial(_ffn_sample_body, final=final), "ffn_sample", outs,
                           x, nw, w_up, conv_w, conv_b, w_down, fnw, past[:, 0], past[:, 1])
    return x_new, jnp.stack([past[:, 1], u], axis=1)


def _ssd_inproj_sample_body(x_ref, nw_ref, wz_ref, wx_ref, wdt_ref, cw_ref, cb_ref, dtb_ref,
                            p0_ref, p1_ref, p2_ref, z_ref, xbc_ref, dt_ref, u_ref):
    h = _rms(x_ref[...], nw_ref[...]).astype(BF16)
    z_ref[...] = jnp.dot(h, wz_ref[...], preferred_element_type=F32)
    dt_ref[...] = _softplus(jnp.dot(h, wdt_ref[...], preferred_element_type=F32) + dtb_ref[...])
    u = jnp.dot(h, wx_ref[...], preferred_element_type=F32)
    u_ref[...] = u
    conv = _conv_step(u, cw_ref[...], [p0_ref[...], p1_ref[...], p2_ref[...]])
    xbc_ref[...] = _silu(conv + cb_ref[...])


def _ssd_state_sample_body(xbc_ref, dt_ref, alog_ref, dskip_ref, st_ref, y_ref, nst_ref):
    n = SSD_STATE
    xbc = xbc_ref[0]
    dt = dt_ref[0]
    e_dec = jnp.exp(dt * (-jnp.exp(alog_ref[...])))
    row = lax.broadcasted_iota(jnp.int32, (n, n), 0)
    col = lax.broadcasted_iota(jnp.int32, (n, n), 1)
    row_lo = row < SSD_HEAD_DIM
    for j in range(SSD_PAIRS):
        g = j // (SSD_PAIRS // SSD_GROUPS)
        h0 = 2 * j
        b_row = xbc[:, D_INNER + g * n:D_INNER + (g + 1) * n]
        c_row = xbc[:, D_INNER + (SSD_GROUPS + g) * n:D_INNER + (SSD_GROUPS + g + 1) * n]
        xs = xbc[:, j * LANES:(j + 1) * LANES]
        xdt = xs * _pair_lanes(dt, h0, (1, LANES))
        state = st_ref[0, j]
        y = jnp.sum(c_row * b_row, axis=1, keepdims=True) * xdt
        c_rows = jnp.broadcast_to(c_row, (SUBLANES, n))
        y = y + _bdot_nt(c_rows, state)[0:1, :] * _pair_lanes(e_dec, h0, (1, LANES))
        y_ref[0, :, j * LANES:(j + 1) * LANES] = y + xs * dskip_ref[:, j * LANES:(j + 1) * LANES]
        diag = jnp.where(row == col, jnp.broadcast_to(xdt, (n, n)), 0.0)
        upd = _bdot(diag, jnp.broadcast_to(b_row, (n, n)))
        scale = jnp.where(row_lo, e_dec[:, h0:h0 + 1], e_dec[:, h0 + 1:h0 + 2])
        nst_ref[0, j] = state * scale + upd


def _ssd_out_sample_body(x_ref, y_ref, z_ref, nw_ref, wout_ref, o_ref):
    yn = _group_norm_gate(y_ref[...], z_ref[...], nw_ref[...])
    o_ref[...] = x_ref[...] + _bdot(yn, wout_ref[...])


def _ssd_sample(x, nw, w_z, w_x, w_dt, conv_w, conv_b, dt_bias, a_log, d_skip, norm_w, w_out,
                state, conv_past):
    r = DEC_BATCH
    outs = [jax.ShapeDtypeStruct((r, D_INNER), F32), jax.ShapeDtypeStruct((r, SSD_CONV_DIM), F32),
            jax.ShapeDtypeStruct((r, LANES), F32), jax.ShapeDtypeStruct((r, SSD_CONV_DIM), F32)]
    z, xbc, dt, u = _call_whole(_ssd_inproj_sample_body, "ssd_inproj_sample", outs,
                                x, nw, w_z, w_x, w_dt, conv_w, conv_b, dt_bias,
                                conv_past[:, 0], conv_past[:, 1], conv_past[:, 2])
    new_conv = jnp.concatenate([conv_past[:, 1:], u[:, None]], axis=1)
    st_shape = (r, SSD_PAIRS, 2 * SSD_HEAD_DIM, SSD_STATE)
    st_spec = pl.BlockSpec((1,) + st_shape[1:], lambda b: (b, 0, 0, 0))
    vec = lambda w: pl.BlockSpec((1, 1, w), lambda b: (b, 0, 0))
    const = lambda w: pl.BlockSpec((1, w), lambda b: (0, 0))
    y, new_state = pl.pallas_call(
        _ssd_state_sample_body,
        grid=(r,),
        in_specs=[vec(SSD_CONV_DIM), vec(LANES), const(LANES), const(D_INNER), st_spec],
        out_specs=[vec(D_INNER), st_spec],
        out_shape=[jax.ShapeDtypeStruct((r, 1, D_INNER), F32), jax.ShapeDtypeStruct(st_shape, F32)],
        compiler_params=_params("arbitrary"),
        name="ssd_state_sample",
    )(xbc.reshape(r, 1, SSD_CONV_DIM), dt.reshape(r, 1, LANES), a_log, d_skip,
      state.reshape(st_shape))
    (x_new,) = _call_whole(_ssd_out_sample_body, "ssd_out_sample",
                           [jax.ShapeDtypeStruct((r, D_MODEL), F32)],
                           x, y.reshape(r, D_INNER), z, norm_w, w_out)
    return x_new, new_conv, new_state.reshape(r, SSD_HEADS, SSD_HEAD_DIM, SSD_STATE)


def _qkv_sample_body(x_ref, nw_ref, wt_ref, cos_ref, sdn_ref, sup_ref, qt_ref, kt_ref, vt_ref, sown_ref):
    rows = x_ref.shape[0]
    tab = lambda r: jnp.broadcast_to(r[...], (ATTN_HEAD_DIM, rows))
    qt, kt, vt = _qkv_t(x_ref[...], nw_ref[...], wt_ref, tab(cos_ref), tab(sdn_ref), tab(sup_ref))
    qt_ref[...] = qt
    kt_ref[...] = kt
    vt_ref[...] = vt
    per_head = (qt * kt).reshape(ATTN_HEADS, ATTN_HEAD_DIM, rows)
    sown_ref[...] = jnp.sum(per_head, axis=1) * ATTN_SCALE


N_PAST_BLK = PAST_LEN // MOBA_BLOCK
PAGES_PER_BLK = MOBA_BLOCK // PAGE_SIZE
HEAD3 = (ATTN_HEADS, ATTN_HEAD_DIM, LANES)


def _moba_sample_body(pt_ref, qt_ref, vnt_ref, sown_ref, *refs):
    del pt_ref
    k_refs, v_refs = refs[:N_PAGES], refs[N_PAGES:2 * N_PAGES]
    ot_ref, s_ref, p_ref = refs[2 * N_PAGES:]
    b = pl.program_id(0)
    blk = MOBA_BLOCK
    is_seq = lax.broadcasted_iota(jnp.int32, (D_MODEL, LANES), 1) == b
    lane = lax.broadcasted_iota(jnp.int32, (ATTN_HEADS, LANES), 1)

    @pl.when(b == 0)
    def _():
        ot_ref[...] = jnp.zeros_like(ot_ref)

    def column(src_ref):
        col = jnp.sum(jnp.where(is_seq, src_ref[...], 0.0), axis=1, keepdims=True)
        return jnp.broadcast_to(col, (D_MODEL, LANES)).reshape(HEAD3)

    qb = column(qt_ref)
    gate = jnp.zeros((ATTN_HEADS, LANES), F32)
    for j in range(N_PAST_BLK):
        raw_sum = jnp.zeros((ATTN_HEADS, 1), F32)
        for w in range(PAGES_PER_BLK):
            page = j * PAGES_PER_BLK + w
            s = jnp.sum(k_refs[page][0] * qb, axis=1)
            s_ref[:, page * PAGE_SIZE:(page + 1) * PAGE_SIZE] = s * ATTN_SCALE
            raw_sum = raw_sum + jnp.sum(s, axis=1, keepdims=True)
        gate = jnp.where(lane == j, raw_sum / blk, gate)

    s_own = jnp.sum(jnp.where(lane == b, sown_ref[...], 0.0), axis=1, keepdims=True)
    m = s_own
    masked = []
    for j in range(N_PAST_BLK):
        sel = _moba_select(gate, j, N_PAST_BLK)
        sb = jnp.where(sel, s_ref[:, j * blk:(j + 1) * blk], NEG_INF)
        masked.append(sb)
        m = jnp.maximum(m, jnp.max(sb, axis=1, keepdims=True))
    p_own = jnp.exp(s_own - m)
    l = p_own
    ps = []
    for sb in masked:
        pb = jnp.exp(sb - m)
        ps.append(pb)
        l = l + jnp.sum(pb, axis=1, keepdims=True)
    for j in range(N_PAST_BLK):
        p_ref[:, j * blk:(j + 1) * blk] = ps[j] / l

    first_lane = lax.broadcasted_iota(jnp.int32, HEAD3, 2) == 0
    w_own = jnp.broadcast_to(p_own / l, (ATTN_HEADS, LANES)).reshape(ATTN_HEADS, 1, LANES)
    acc = jnp.where(first_lane, w_own * column(vnt_ref), 0.0)
    for page in range(N_PAGES):
        p = p_ref[:, page * PAGE_SIZE:(page + 1) * PAGE_SIZE]
        acc = acc + p.reshape(ATTN_HEADS, 1, PAGE_SIZE) * v_refs[page][0]
    o_col = jnp.sum(acc, axis=2, keepdims=True)
    o_full = jnp.broadcast_to(o_col, HEAD3).reshape(D_MODEL, LANES)
    ot_ref[...] = jnp.where(is_seq, o_full, ot_ref[...])


def _moba_sample(qt, vnt, s_own, cache_kt, cache_vt, page_table):
    r = DEC_BATCH
    whole2 = pl.BlockSpec((D_MODEL, r), lambda b, pt: (0, 0))
    page = lambda w: pl.BlockSpec((1,) + HEAD3, lambda b, pt: (pt[b * N_PAGES + w], 0, 0, 0))
    pages = [page(w) for w in range(N_PAGES)]
    grid_spec = pltpu.PrefetchScalarGridSpec(
        num_scalar_prefetch=1,
        grid=(r,),
        in_specs=[whole2, whole2, pl.BlockSpec((ATTN_HEADS, r), lambda b, pt: (0, 0))] + pages + pages,
        out_specs=whole2,
        scratch_shapes=[
            pltpu.VMEM((ATTN_HEADS, PAST_LEN), F32),
            pltpu.VMEM((ATTN_HEADS, PAST_LEN), F32),
        ],
    )
    return pl.pallas_call(
        _moba_sample_body,
        grid_spec=grid_spec,
        out_shape=jax.ShapeDtypeStruct((D_MODEL, r), F32),
        compiler_params=_params("arbitrary"),
        name="moba_sample",
    )(page_table.reshape(-1), qt, vnt, s_own, *([cache_kt] * N_PAGES), *([cache_vt] * N_PAGES))


def _rope_tables(pos):
    half = ROT_DIM // 2
    inv_freq = ROPE_THETA ** (-(jnp.arange(half, dtype=F32) * 2.0) / ROT_DIM)
    ang = pos.astype(F32)[:, None] * inv_freq
    cos, sin = jnp.cos(ang), jnp.sin(ang)
    ones = jnp.ones((pos.shape[0], ATTN_HEAD_DIM - ROT_DIM), F32)
    zeros = jnp.zeros((pos.shape[0], ATTN_HEAD_DIM - ROT_DIM), F32)
    zh = jnp.zeros_like(sin)
    return (jnp.concatenate([cos, cos, ones], axis=1),
            jnp.concatenate([-sin, zh, zeros], axis=1),
            jnp.concatenate([zh, sin, zeros], axis=1))


def _pad_lanes(a):
    return jnp.pad(a, [(0, 0)] * (a.ndim - 1) + [(0, LANES - a.shape[-1])])


def kernel(x_prompt, x_sample, state_sconv, state_ssm, state_ssm_conv, cache_k, cache_v, state_ffn_conv, page_table, norm_mix_w, norm_ffn_w, norm_final_w, sconv_w_in, sconv_conv_w, sconv_w_out, ssd_w_in, ssd_conv_w, ssd_conv_b, ssd_dt_bias, ssd_a_log, ssd_d, ssd_norm_w, ssd_w_out, attn_w_qkv, attn_w_o, ffn_w_up, ffn_conv_w, ffn_conv_b, ffn_w_down):
    xp = x_prompt.reshape(N_TOK, D_MODEL)
    xs = x_sample.reshape(DEC_BATCH, D_MODEL)
    fnw = norm_final_w.reshape(1, D_MODEL)
    sconv_p, sconv_s, ffnc_p, ffnc_s = [], [], [], []
    for i in range(DEPTH):
        kind, j = i % N_MIXERS, i // N_MIXERS
        nw = norm_mix_w[i].reshape(1, D_MODEL)
        if kind == 0:
            w_in, w_out = sconv_w_in[j].astype(BF16), sconv_w_out[j].astype(BF16)
            xp, st = _sconv_prompt(xp, nw, w_in, sconv_conv_w[j], w_out)
            xs, st_s = _sconv_sample(xs, nw, w_in, sconv_conv_w[j], w_out, state_sconv[j])
            sconv_p.append(st)
            sconv_s.append(st_s)
        elif kind == 1:
            w_in = ssd_w_in[j]
            w_z = w_in[:, :D_INNER].astype(BF16)
            w_x = w_in[:, D_INNER:D_INNER + SSD_CONV_DIM].astype(BF16)
            w_dt = _pad_lanes(w_in[:, D_INNER + SSD_CONV_DIM:]).astype(BF16)
            w_out = ssd_w_out[j].astype(BF16)
            conv_b = ssd_conv_b[j].reshape(1, SSD_CONV_DIM)
            dt_bias = _pad_lanes(ssd_dt_bias[j].reshape(1, SSD_HEADS))
            a_log = _pad_lanes(ssd_a_log[j].reshape(1, SSD_HEADS))
            d_skip = jnp.repeat(ssd_d[j], SSD_HEAD_DIM).reshape(1, D_INNER)
            norm_w = ssd_norm_w[j].reshape(1, D_INNER)
            z, xbc, dt, ssmc_p = _ssd_inproj_prompt(xp, nw, w_z, w_x, w_dt, ssd_conv_w[j], conv_b, dt_bias)
            d_skip_rows = jnp.broadcast_to(d_skip.reshape(D_INNER, 1), (D_INNER, LANES))
            xp, ssm_p = _ssd_scan_prompt(xp, z, xbc, dt, a_log, d_skip_rows, norm_w, w_out)
            ssm_p = ssm_p.reshape(BATCH, SSD_HEADS, SSD_HEAD_DIM, SSD_STATE)
            xs, ssmc_s, ssm_s = _ssd_sample(xs, nw, w_z, w_x, w_dt, ssd_conv_w[j], conv_b, dt_bias,
                                            a_log, d_skip, norm_w, w_out, state_ssm[j], state_ssm_conv[j])
        else:
            w_qkv, w_o = attn_w_qkv[j].astype(BF16), attn_w_o[j].astype(BF16)
            w_qkv_t = w_qkv.T
            tabs = [t.T for t in _rope_tables(jnp.arange(SEQ))]
            qt, kt_p, vt_p, k_rows = _qkv_prompt(xp, nw, w_qkv_t, *tabs)
            xp = _proj_residual_t(_moba_prompt(qt, k_rows, vt_p), w_o, xp, PROJ_TL)
            tabs = [t.T for t in _rope_tables(jnp.full((1,), PAST_LEN))]
            sds = jax.ShapeDtypeStruct((D_MODEL, DEC_BATCH), F32)
            own = jax.ShapeDtypeStruct((ATTN_HEADS, DEC_BATCH), F32)
            qt_s, kt_s, vt_s, s_own = _call_whole(_qkv_sample_body, "qkv_sample", [sds, sds, sds, own],
                                                  xs, nw, w_qkv_t, *tabs)
            pages = lambda c: jnp.transpose(c, (0, 2, 3, 1))
            ot_s = _moba_sample(qt_s, vt_s, s_own, pages(cache_k[j]), pages(cache_v[j]), page_table)
            xs = _proj_residual_t(ot_s[None], w_o, xs, DEC_BATCH)
        nwf = norm_ffn_w[i].reshape(1, D_MODEL)
        w_up, w_down = ffn_w_up[i].astype(BF16), ffn_w_down[i].astype(BF16)
        conv_b = ffn_conv_b[i].reshape(1, 2 * D_FF)
        final = i == DEPTH - 1
        xp, fc_p = _ffn_prompt(xp, nwf, w_up, ffn_conv_w[i], conv_b, w_down, fnw, final)
        xs, fc_s = _ffn_sample(xs, nwf, w_up, ffn_conv_w[i], conv_b, w_down, fnw, state_ffn_conv[i], final)
        ffnc_p.append(fc_p)
        ffnc_s.append(fc_s)
    heads = (ATTN_HEADS, ATTN_HEAD_DIM)
    kv_p = lambda t: jnp.transpose(t.reshape((BATCH,) + heads + (SEQ,)), (0, 3, 1, 2))[None]
    kv_s = lambda t: jnp.transpose(t.reshape(heads + (DEC_BATCH,)), (2, 0, 1))[None, :, None]
    return (xp.reshape(BATCH, SEQ, D_MODEL), xs.reshape(DEC_BATCH, 1, D_MODEL),
            jnp.stack(sconv_p), jnp.stack(sconv_s),
            ssm_p[None], ssm_s[None], ssmc_p[None], ssmc_s[None],
            kv_p(kt_p), kv_p(vt_p), kv_s(kt_s), kv_s(vt_s),
            jnp.stack(ffnc_p), jnp.stack(ffnc_s))
```

```python
import functools

import jax
import jax.numpy as jnp
from jax import lax
from jax.experimental import pallas as pl
from jax.experimental.pallas import tpu as pltpu

F32 = jnp.float32
BF16 = jnp.bfloat16

D_MODEL = 1024
BATCH = 8
SEQ = 2048
DEPTH = 4
DEC_BATCH = 128
PAST_LEN = 2048
PAGE_SIZE = 128
N_PAGES = PAST_LEN // PAGE_SIZE
N_MIXERS = 3
RMS_EPS = 1e-6
SCONV_WIDTH = 3
D_INNER = 2 * D_MODEL
SSD_HEAD_DIM = 64
SSD_HEADS = D_INNER // SSD_HEAD_DIM
SSD_GROUPS = 4
SSD_STATE = 128
SSD_CONV_WIDTH = 4
SSD_CONV_DIM = D_INNER + 2 * SSD_GROUPS * SSD_STATE
SSD_CHUNK = 128
SSD_PAIRS = SSD_HEADS // 2
ATTN_HEAD_DIM = 64
ATTN_HEADS = D_MODEL // ATTN_HEAD_DIM
MOBA_BLOCK = 256
MOBA_TOP_K = 3
ROPE_THETA = 500000.0
ROT_DIM = ATTN_HEAD_DIM // 4
D_FF = 2816
FFN_CONV_WIDTH = 3

LANES = 128
SUBLANES = 8
VMEM_LIMIT = 56 * 1024 * 1024
N_TOK = BATCH * SEQ
NEG_INF = float("-inf")


def _params(*sem):
    return pltpu.CompilerParams(dimension_semantics=sem, vmem_limit_bytes=VMEM_LIMIT)


def _bdot(a, b):
    return jnp.dot(a.astype(BF16), b.astype(BF16), preferred_element_type=F32)


def _bdot_nt(a, b):
    return lax.dot_general(a.astype(BF16), b.astype(BF16), (((1,), (1,)), ((), ())),
                           preferred_element_type=F32)


def _rms(x, w):
    inv = lax.rsqrt(jnp.mean(x * x, axis=-1, keepdims=True) + RMS_EPS)
    return (x * inv) * w


def _silu(x):
    return x * (1.0 / (1.0 + jnp.exp(-x)))


def _softplus(x):
    return jnp.maximum(x, 0.0) + jnp.log1p(jnp.exp(-jnp.abs(x)))


def _conv_rows(u, w, prev):
    width = w.shape[0]
    row8 = lax.broadcasted_iota(jnp.int32, (SUBLANES, u.shape[1]), 0)
    out = u * w[width - 1:width, :]
    for k in range(1, width):
        sh = pltpu.roll(u, k, axis=0)
        head = sh[:SUBLANES]
        for t in range(k):
            src = SUBLANES - k + t
            head = jnp.where(row8 == t, prev[src:src + 1, :], head)
        sh = jnp.concatenate([head, sh[SUBLANES:]], axis=0)
        out = out + sh * w[width - 1 - k:width - k, :]
    return out


def _conv_step(u, w, past):
    width = w.shape[0]
    out = u * w[width - 1:width, :]
    for k in range(width - 1):
        out = out + past[k] * w[k:k + 1, :]
    return out


SCONV_TL = 1024


def _sconv_prompt_body(x_ref, nw_ref, win_ref, cw_ref, wout_ref, o_ref, st_ref, carry_ref):
    l = pl.program_id(1)

    @pl.when(l == 0)
    def _():
        carry_ref[...] = jnp.zeros_like(carry_ref)

    x = x_ref[...]
    h = _rms(x, nw_ref[...]).astype(BF16)
    p = jnp.dot(h, win_ref[...], preferred_element_type=F32)
    g = p[:, D_MODEL:2 * D_MODEL] * p[:, 2 * D_MODEL:]
    c = _conv_rows(g, cw_ref[...], carry_ref[...])
    carry_ref[...] = g[SCONV_TL - SUBLANES:, :]
    o_ref[...] = x + _bdot(p[:, :D_MODEL] * c, wout_ref[...])

    @pl.when(l == pl.num_programs(1) - 1)
    def _():
        st_ref[0] = g[SCONV_TL - (SCONV_WIDTH - 1):, :]


def _sconv_prompt(x, nw, w_in, conv_w, w_out):
    nl = SEQ // SCONV_TL
    row = lambda b, l: (b * nl + l, 0)
    const = lambda b, l: (0, 0)
    return pl.pallas_call(
        _sconv_prompt_body,
        grid=(BATCH, nl),
        in_specs=[
            pl.BlockSpec((SCONV_TL, D_MODEL), row),
            pl.BlockSpec((1, D_MODEL), const),
            pl.BlockSpec((D_MODEL, 3 * D_MODEL), const),
            pl.BlockSpec((SCONV_WIDTH, D_MODEL), const),
            pl.BlockSpec((D_MODEL, D_MODEL), const),
        ],
        out_specs=[
            pl.BlockSpec((SCONV_TL, D_MODEL), row),
            pl.BlockSpec((1, SCONV_WIDTH - 1, D_MODEL), lambda b, l: (b, 0, 0)),
        ],
        out_shape=[
            jax.ShapeDtypeStruct((N_TOK, D_MODEL), F32),
            jax.ShapeDtypeStruct((BATCH, SCONV_WIDTH - 1, D_MODEL), F32),
        ],
        scratch_shapes=[pltpu.VMEM((SUBLANES, D_MODEL), F32)],
        compiler_params=_params("arbitrary", "arbitrary"),
        name="sconv_prompt",
    )(x, nw, w_in, conv_w, w_out)


FFN_TL = 1024
FFN_CHUNK = D_FF // 2


def _ffn_prompt_body(x_ref, nw_ref, wup_ref, cw_ref, cb_ref, wdn_ref, fnw_ref, o_ref, st_ref,
                     carry_ref, *, final):
    l = pl.program_id(1)

    @pl.when(l == 0)
    def _():
        carry_ref[...] = jnp.zeros_like(carry_ref)

    x = x_ref[...]
    h = _rms(x, nw_ref[...]).astype(BF16)
    acc = x
    last = l == pl.num_programs(1) - 1
    for c in range(D_FF // FFN_CHUNK):
        halves = []
        for base in (0, D_FF):
            lo = base + c * FFN_CHUNK
            hi = lo + FFN_CHUNK
            u = jnp.dot(h, wup_ref[:, lo:hi], preferred_element_type=F32)
            conv = _conv_rows(u, cw_ref[:, lo:hi], carry_ref[:, lo:hi]) + cb_ref[:, lo:hi]
            carry_ref[:, lo:hi] = u[FFN_TL - SUBLANES:, :]

            @pl.when(last)
            def _(u=u, lo=lo, hi=hi):
                st_ref[0, :, lo:hi] = u[FFN_TL - (FFN_CONV_WIDTH - 1):, :]

            halves.append(conv)
        a, g = halves
        acc = acc + _bdot(_silu(g) * a, wdn_ref[c * FFN_CHUNK:(c + 1) * FFN_CHUNK, :])
    if final:
        acc = _rms(acc, fnw_ref[...])
    o_ref[...] = acc


def _ffn_prompt(x, nw, w_up, conv_w, conv_b, w_down, fnw, final):
    nl = SEQ // FFN_TL
    row = lambda b, l: (b * nl + l, 0)
    const = lambda b, l: (0, 0)
    once = pl.Buffered(1)
    return pl.pallas_call(
        functools.partial(_ffn_prompt_body, final=final),
        grid=(BATCH, nl),
        in_specs=[
            pl.BlockSpec((FFN_TL, D_MODEL), row),
            pl.BlockSpec((1, D_MODEL), const),
            pl.BlockSpec((D_MODEL, 2 * D_FF), const, pipeline_mode=once),
            pl.BlockSpec((FFN_CONV_WIDTH, 2 * D_FF), const),
            pl.BlockSpec((1, 2 * D_FF), const),
            pl.BlockSpec((D_FF, D_MODEL), const, pipeline_mode=once),
            pl.BlockSpec((1, D_MODEL), const),
        ],
        out_specs=[
            pl.BlockSpec((FFN_TL, D_MODEL), row),
            pl.BlockSpec((1, FFN_CONV_WIDTH - 1, 2 * D_FF), lambda b, l: (b, 0, 0)),
        ],
        out_shape=[
            jax.ShapeDtypeStruct((N_TOK, D_MODEL), F32),
            jax.ShapeDtypeStruct((BATCH, FFN_CONV_WIDTH - 1, 2 * D_FF), F32),
        ],
        scratch_shapes=[pltpu.VMEM((SUBLANES, 2 * D_FF), F32)],
        compiler_params=_params("arbitrary", "arbitrary"),
        name="ffn_prompt",
    )(x, nw, w_up, conv_w, conv_b, w_down, fnw)


SSD_TL = 256


def _ssd_inproj_prompt_body(x_ref, nw_ref, wz_ref, wx_ref, wdt_ref, cw_ref, cb_ref, dtb_ref,
                            z_ref, xbc_ref, dt_ref, st_ref, carry_ref):
    l = pl.program_id(1)

    @pl.when(l == 0)
    def _():
        carry_ref[...] = jnp.zeros_like(carry_ref)

    h = _rms(x_ref[...], nw_ref[...]).astype(BF16)
    z_ref[...] = jnp.dot(h, wz_ref[...], preferred_element_type=F32)
    dt_ref[...] = _softplus(jnp.dot(h, wdt_ref[...], preferred_element_type=F32) + dtb_ref[...])
    u = jnp.dot(h, wx_ref[...], preferred_element_type=F32)
    xbc_ref[...] = _silu(_conv_rows(u, cw_ref[...], carry_ref[...]) + cb_ref[...])
    carry_ref[...] = u[SSD_TL - SUBLANES:, :]

    @pl.when(l == pl.num_programs(1) - 1)
    def _():
        st_ref[0] = u[SSD_TL - (SSD_CONV_WIDTH - 1):, :]


def _ssd_inproj_prompt(x, nw, w_z, w_x, w_dt, conv_w, conv_b, dt_bias):
    nl = SEQ // SSD_TL
    row = lambda b, l: (b * nl + l, 0)
    const = lambda b, l: (0, 0)
    return pl.pallas_call(
        _ssd_inproj_prompt_body,
        grid=(BATCH, nl),
        in_specs=[
            pl.BlockSpec((SSD_TL, D_MODEL), row),
            pl.BlockSpec((1, D_MODEL), const),
            pl.BlockSpec((D_MODEL, D_INNER), const),
            pl.BlockSpec((D_MODEL, SSD_CONV_DIM), const),
            pl.BlockSpec((D_MODEL, LANES), const),
            pl.BlockSpec((SSD_CONV_WIDTH, SSD_CONV_DIM), const),
            pl.BlockSpec((1, SSD_CONV_DIM), const),
            pl.BlockSpec((1, LANES), const),
        ],
        out_specs=[
            pl.BlockSpec((SSD_TL, D_INNER), row),
            pl.BlockSpec((SSD_TL, SSD_CONV_DIM), row),
            pl.BlockSpec((SSD_TL, LANES), row),
            pl.BlockSpec((1, SSD_CONV_WIDTH - 1, SSD_CONV_DIM), lambda b, l: (b, 0, 0)),
        ],
        out_shape=[
            jax.ShapeDtypeStruct((N_TOK, D_INNER), F32),
            jax.ShapeDtypeStruct((N_TOK, SSD_CONV_DIM), F32),
            jax.ShapeDtypeStruct((N_TOK, LANES), F32),
            jax.ShapeDtypeStruct((BATCH, SSD_CONV_WIDTH - 1, SSD_CONV_DIM), F32),
        ],
        scratch_shapes=[pltpu.VMEM((SUBLANES, SSD_CONV_DIM), F32)],
        compiler_params=_params("arbitrary", "arbitrary"),
        name="ssd_inproj_prompt",
    )(x, nw, w_z, w_x, w_dt, conv_w, conv_b, dt_bias)


def _cumsum_rows(x):
    n = x.shape[0]
    row = lax.broadcasted_iota(jnp.int32, x.shape, 0)
    k = 1
    while k < n:
        x = x + jnp.where(row >= k, pltpu.roll(x, k, axis=0), 0.0)
        k *= 2
    return x


def _pair_lanes(vals, h0, shape):
    lane = lax.broadcasted_iota(jnp.int32, shape, 1)
    return jnp.where(lane < SSD_HEAD_DIM, vals[:, h0:h0 + 1], vals[:, h0 + 1:h0 + 2])


def _group_norm_gate(y, z, nw):
    y = y * _silu(z)
    gw = D_INNER // SSD_GROUPS
    outs = []
    for g in range(SSD_GROUPS):
        yg = y[:, g * gw:(g + 1) * gw]
        outs.append(_rms(yg, nw[:, g * gw:(g + 1) * gw]))
    return jnp.concatenate(outs, axis=1)


def _ssd_scan_prompt_body(x_ref, z_ref, xbc_ref, dt_ref, alog_ref, dskip_ref, nw_ref, wout_ref,
                          o_ref, st_ref, y_ref):
    c = pl.program_id(1)

    @pl.when(c == 0)
    def _():
        st_ref[...] = jnp.zeros_like(st_ref)

    q = SSD_CHUNK
    p2 = 2 * SSD_HEAD_DIM
    dt = dt_ref[...]
    acum = _cumsum_rows(dt * (-jnp.exp(alog_ref[...])))
    last = acum[q - 1:q, :]
    acum_t = acum.T
    dt_t = dt.T
    w_t = (dt * jnp.exp(last - acum)).T
    e_acum_t = jnp.exp(acum_t)
    e_chunk = jnp.exp(last)
    row = lax.broadcasted_iota(jnp.int32, (q, q), 0)
    col = lax.broadcasted_iota(jnp.int32, (q, q), 1)
    causal_t = col >= row
    gp = SSD_PAIRS // SSD_GROUPS
    gw = gp * p2
    head_of_row = lax.broadcasted_iota(jnp.int32, (gw, q), 0) // SSD_HEAD_DIM
    head_of_state = lax.broadcasted_iota(jnp.int32, (gw, SSD_STATE), 0) // SSD_HEAD_DIM

    def group_rows(vals_t, g):
        h_first = g * 2 * gp
        out = jnp.broadcast_to(vals_t[h_first:h_first + 1, :], (gw, q))
        for e in range(1, 2 * gp):
            out = jnp.where(head_of_row >= e, vals_t[h_first + e:h_first + e + 1, :], out)
        return out

    for g in range(SSD_GROUPS):
        b_g = xbc_ref[:, D_INNER + g * SSD_STATE:D_INNER + (g + 1) * SSD_STATE].astype(BF16)
        c_g = xbc_ref[:, D_INNER + (SSD_GROUPS + g) * SSD_STATE:D_INNER + (SSD_GROUPS + g + 1) * SSD_STATE]
        cb_t = _bdot_nt(b_g, c_g)
        c_t = c_g.T.astype(BF16)
        xs_t = xbc_ref[:, g * gw:(g + 1) * gw].T
        xdt_t = (xs_t * group_rows(dt_t, g)).astype(BF16)
        yd = []
        for e in range(2 * gp):
            h = g * 2 * gp + e
            seg = acum_t[h:h + 1, :] - acum[:, h:h + 1]
            decay = jnp.exp(jnp.where(causal_t, seg, NEG_INF))
            yd.append(_bdot(xdt_t[e * SSD_HEAD_DIM:(e + 1) * SSD_HEAD_DIM, :], cb_t * decay))
        state = st_ref[0, g * gp:(g + 1) * gp].reshape(gw, SSD_STATE)
        y_t = jnp.concatenate(yd, axis=0) + _bdot(state, c_t) * group_rows(e_acum_t, g)
        y_t = y_t + xs_t * dskip_ref[g * gw:(g + 1) * gw, :]
        y_ref[:, g * gw:(g + 1) * gw] = y_t.T
        upd = _bdot(xs_t * group_rows(w_t, g), b_g)
        scale = jnp.broadcast_to(e_chunk[:, g * 2 * gp:g * 2 * gp + 1], (gw, SSD_STATE))
        for e in range(1, 2 * gp):
            scale = jnp.where(head_of_state >= e, e_chunk[:, g * 2 * gp + e:g * 2 * gp + e + 1], scale)
        st_ref[0, g * gp:(g + 1) * gp] = (state * scale + upd).reshape(gp, p2, SSD_STATE)

    yn = _group_norm_gate(y_ref[...], z_ref[...], nw_ref[...])
    o_ref[...] = x_ref[...] + _bdot(yn, wout_ref[...])


def _ssd_scan_prompt(x, z, xbc, dt, a_log, d_skip, nw, w_out):
    nc = SEQ // SSD_CHUNK
    row = lambda b, c: (b * nc + c, 0)
    const = lambda b, c: (0, 0)
    return pl.pallas_call(
        _ssd_scan_prompt_body,
        grid=(BATCH, nc),
        in_specs=[
            pl.BlockSpec((SSD_CHUNK, D_MODEL), row),
            pl.BlockSpec((SSD_CHUNK, D_INNER), row),
            pl.BlockSpec((SSD_CHUNK, SSD_CONV_DIM), row),
            pl.BlockSpec((SSD_CHUNK, LANES), row),
            pl.BlockSpec((1, LANES), const),
            pl.BlockSpec((D_INNER, LANES), const),
            pl.BlockSpec((1, D_INNER), const),
            pl.BlockSpec((D_INNER, D_MODEL), const),
        ],
        out_specs=[
            pl.BlockSpec((SSD_CHUNK, D_MODEL), row),
            pl.BlockSpec((1, SSD_PAIRS, 2 * SSD_HEAD_DIM, SSD_STATE), lambda b, c: (b, 0, 0, 0)),
        ],
        out_shape=[
            jax.ShapeDtypeStruct((N_TOK, D_MODEL), F32),
            jax.ShapeDtypeStruct((BATCH, SSD_PAIRS, 2 * SSD_HEAD_DIM, SSD_STATE), F32),
        ],
        scratch_shapes=[pltpu.VMEM((SSD_CHUNK, D_INNER), F32)],
        compiler_params=_params("arbitrary", "arbitrary"),
        name="ssd_scan_prompt",
    )(x, z, xbc, dt, a_log, d_skip, nw, w_out)


QKV_TL = 512


def _rope_t(t, cos, sin_dn, sin_up):
    reps = t.shape[0] // ATTN_HEAD_DIM
    half = ROT_DIM // 2
    tile = lambda a: jnp.concatenate([a] * reps, axis=0)
    return (t * tile(cos)
            + pltpu.roll(t, t.shape[0] - half, axis=0) * tile(sin_dn)
            + pltpu.roll(t, half, axis=0) * tile(sin_up))


def _qkv_t(x, nw, wt_ref, cos, sdn, sup):
    ht = _rms(x, nw).T.astype(BF16)
    qt = _rope_t(jnp.dot(wt_ref[:D_MODEL, :], ht, preferred_element_type=F32), cos, sdn, sup)
    kt = _rope_t(jnp.dot(wt_ref[D_MODEL:2 * D_MODEL, :], ht, preferred_element_type=F32), cos, sdn, sup)
    vt = jnp.dot(wt_ref[2 * D_MODEL:, :], ht, preferred_element_type=F32)
    return qt, kt, vt


def _qkv_prompt_body(x_ref, nw_ref, wt_ref, cos_ref, sdn_ref, sup_ref, qt_ref, kt_ref, vt_ref, k_ref):
    qt, kt, vt = _qkv_t(x_ref[...], nw_ref[...], wt_ref, cos_ref[...], sdn_ref[...], sup_ref[...])
    qt_ref[0] = qt
    kt_ref[0] = kt
    vt_ref[0] = vt
    k_ref[...] = kt.T


def _qkv_prompt(x, nw, w_qkv_t, cos, sdn, sup):
    nl = SEQ // QKV_TL
    row = lambda b, l: (b * nl + l, 0)
    const = lambda b, l: (0, 0)
    tab = pl.BlockSpec((ATTN_HEAD_DIM, QKV_TL), lambda b, l: (0, l))
    tspec = pl.BlockSpec((1, D_MODEL, QKV_TL), lambda b, l: (b, 0, l))
    tout = jax.ShapeDtypeStruct((BATCH, D_MODEL, SEQ), F32)
    return pl.pallas_call(
        _qkv_prompt_body,
        grid=(BATCH, nl),
        in_specs=[
            pl.BlockSpec((QKV_TL, D_MODEL), row),
            pl.BlockSpec((1, D_MODEL), const),
            pl.BlockSpec((3 * D_MODEL, D_MODEL), const),
            tab, tab, tab,
        ],
        out_specs=[tspec, tspec, tspec, pl.BlockSpec((QKV_TL, D_MODEL), row)],
        out_shape=[tout, tout, tout, jax.ShapeDtypeStruct((N_TOK, D_MODEL), F32)],
        compiler_params=_params("arbitrary", "arbitrary"),
        name="qkv_prompt",
    )(x, nw, w_qkv_t, cos, sdn, sup)


N_BLK = SEQ // MOBA_BLOCK
ATTN_SCALE = ATTN_HEAD_DIM ** -0.5
LOG2_E = 1.4426950408889634


def _moba_select(gate, n, own):
    ax = gate.ndim - 1
    lane = lax.broadcasted_iota(jnp.int32, gate.shape, ax)
    g_n = jnp.sum(jnp.where(lane == n, gate, 0.0), axis=ax, keepdims=True)
    ahead = (gate > g_n) | ((gate == g_n) & (lane < n))
    rank = jnp.sum(jnp.where(ahead & (lane < own), 1.0, 0.0), axis=ax, keepdims=True)
    return rank < MOBA_TOP_K


def _moba_bias_rows(gate_t, own):
    ax = gate_t.ndim - 2
    row = lax.broadcasted_iota(jnp.int32, gate_t.shape, ax)
    past = row < own
    bias = jnp.full(gate_t.shape, NEG_INF, F32)
    for n in range(gate_t.shape[ax] - 1):
        g_n = gate_t[:, n:n + 1, :]
        ahead = (gate_t > g_n) | ((gate_t == g_n) & (row < n))
        rank = jnp.sum(jnp.where(ahead & past, 1.0, 0.0), axis=ax, keepdims=True)
        chosen = (rank < MOBA_TOP_K) & (row == n) & past
        bias = jnp.where(chosen, 0.0, bias)
    return bias


def _dot3(a, b):
    a_hi = a.astype(BF16)
    b_hi = b.astype(BF16)
    a_lo = (a - a_hi.astype(F32)).astype(BF16)
    b_lo = (b - b_hi.astype(F32)).astype(BF16)
    return (jnp.dot(a_hi, b_hi, preferred_element_type=F32)
            + jnp.dot(a_hi, b_lo, preferred_element_type=F32)
            + jnp.dot(a_lo, b_hi, preferred_element_type=F32))


MOBA_HEADS = 4
MOBA_ROWS = MOBA_HEADS * ATTN_HEAD_DIM


def _moba_prompt_body(qt_ref, k_ref, vt_ref, ot_ref, kmean_ref, bias_ref, sa_ref, sb_ref):
    i = pl.program_id(2)
    blk = MOBA_BLOCK
    dh = ATTN_HEAD_DIM
    heads = range(MOBA_HEADS)

    @pl.when(i == 0)
    def _():
        for n in range(N_BLK):
            kmean_ref[n:n + 1, :] = jnp.mean(k_ref[n * blk:(n + 1) * blk, :], axis=0, keepdims=True)

    qt = qt_ref[0]
    q_row = lax.broadcasted_iota(jnp.int32, qt.shape, 0)
    km = kmean_ref[...]
    km_lane = lax.broadcasted_iota(jnp.int32, km.shape, 1)
    qs = qt * (ATTN_SCALE * LOG2_E)
    q_w = [jnp.where((q_row >= hh * dh) & (q_row < (hh + 1) * dh), qs, 0.0).astype(BF16) for hh in heads]
    km_heads = [jnp.where((km_lane >= hh * dh) & (km_lane < (hh + 1) * dh), km, 0.0) for hh in heads]
    gates = _dot3(jnp.concatenate(km_heads, axis=0), qt)
    bias_ref[...] = _moba_bias_rows(gates.reshape(MOBA_HEADS, N_BLK, blk), i)

    key = lax.broadcasted_iota(jnp.int32, (blk, blk), 0)
    qry = lax.broadcasted_iota(jnp.int32, (blk, blk), 1)

    def scores(n, dst_ref):
        k_n = k_ref[pl.ds(pl.multiple_of(n * blk, blk), blk), :].astype(BF16)
        for hh in heads:
            dst_ref[hh] = jnp.dot(k_n, q_w[hh], preferred_element_type=F32)

    def absorb(n, masked, state):
        vt_n = vt_ref[0, :, pl.ds(pl.multiple_of(n * blk, blk), blk)].astype(BF16)
        out = []
        for hh in heads:
            m, l, acc = state[3 * hh:3 * hh + 3]
            s = masked(hh)
            m_new = jnp.maximum(m, jnp.max(s, axis=0, keepdims=True))
            m_ref = jnp.where(m_new == NEG_INF, 0.0, m_new)
            alpha = jnp.exp2(m - m_ref)
            p = jnp.exp2(s - m_ref)
            l = l * alpha + jnp.sum(p, axis=0, keepdims=True)
            acc = acc * alpha + jnp.dot(vt_n[hh * dh:(hh + 1) * dh, :], p.astype(BF16),
                                        preferred_element_type=F32)
            out += [m_new, l, acc]
        return tuple(out)

    def past(n, src_ref):
        return lambda hh: src_ref[hh] + bias_ref[hh, pl.ds(n, 1), :]

    def block_pair(p, state):
        scores(jnp.minimum(2 * p + 1, i), sb_ref)
        state = absorb(2 * p, past(2 * p, sa_ref), state)
        scores(jnp.minimum(2 * p + 2, i), sa_ref)
        return absorb(jnp.minimum(2 * p + 1, i), past(2 * p + 1, sb_ref), state)

    scores(0, sa_ref)
    init = (jnp.full((1, blk), NEG_INF, F32), jnp.zeros((1, blk), F32), jnp.zeros((dh, blk), F32))
    state = lax.fori_loop(0, (i + 1) // 2, block_pair, init * MOBA_HEADS)
    final = absorb(i, lambda hh: jnp.where(key <= qry, sa_ref[hh], NEG_INF), state)
    ot_ref[0] = jnp.concatenate([final[3 * hh + 2] / final[3 * hh + 1] for hh in heads], axis=0)


def _moba_prompt(qt, k, vt):
    groups = ATTN_HEADS // MOBA_HEADS
    qspec = pl.BlockSpec((1, MOBA_ROWS, MOBA_BLOCK), lambda b, p, i: (b, p, i))
    return pl.pallas_call(
        _moba_prompt_body,
        grid=(BATCH, groups, N_BLK),
        in_specs=[
            qspec,
            pl.BlockSpec((SEQ, MOBA_ROWS), lambda b, p, i: (b, p)),
            pl.BlockSpec((1, MOBA_ROWS, SEQ), lambda b, p, i: (b, p, 0)),
        ],
        out_specs=qspec,
        out_shape=jax.ShapeDtypeStruct((BATCH, D_MODEL, SEQ), F32),
        scratch_shapes=[pltpu.VMEM((N_BLK, MOBA_ROWS), F32),
                        pltpu.VMEM((MOBA_HEADS, N_BLK, MOBA_BLOCK), F32),
                        pltpu.VMEM((MOBA_HEADS, MOBA_BLOCK, MOBA_BLOCK), F32),
                        pltpu.VMEM((MOBA_HEADS, MOBA_BLOCK, MOBA_BLOCK), F32)],
        compiler_params=_params("arbitrary", "arbitrary", "arbitrary"),
        name="moba_prompt",
    )(qt, k, vt)


PROJ_TL = 512


def _proj_residual_t_body(at_ref, w_ref, x_ref, o_ref):
    o_ref[...] = x_ref[...] + _bdot(at_ref[0].T, w_ref[...])


def _proj_residual_t(at, w, x, tl):
    groups, kdim, per = at.shape
    nl = per // tl
    row = lambda g, l: (g * nl + l, 0)
    return pl.pallas_call(
        _proj_residual_t_body,
        grid=(groups, nl),
        in_specs=[
            pl.BlockSpec((1, kdim, tl), lambda g, l: (g, 0, l)),
            pl.BlockSpec(w.shape, lambda g, l: (0, 0)),
            pl.BlockSpec((tl, w.shape[1]), row),
        ],
        out_specs=pl.BlockSpec((tl, w.shape[1]), row),
        out_shape=jax.ShapeDtypeStruct(x.shape, F32),
        compiler_params=_params("arbitrary", "arbitrary"),
        name="proj_residual",
    )(at, w, x)


def _whole(shape):
    return pl.BlockSpec(shape, lambda: (0,) * len(shape))


def _call_whole(body, name, out_shapes, *args):
    return pl.pallas_call(
        body,
        in_specs=[_whole(a.shape) for a in args],
        out_specs=[_whole(s.shape) for s in out_shapes],
        out_shape=out_shapes,
        compiler_params=pltpu.CompilerParams(vmem_limit_bytes=VMEM_LIMIT),
        name=name,
    )(*args)


def _sconv_sample_body(x_ref, nw_ref, win_ref, cw_ref, wout_ref, p0_ref, p1_ref, o_ref, g_ref):
    x = x_ref[...]
    h = _rms(x, nw_ref[...]).astype(BF16)
    p = jnp.dot(h, win_ref[...], preferred_element_type=F32)
    g = p[:, D_MODEL:2 * D_MODEL] * p[:, 2 * D_MODEL:]
    c = _conv_step(g, cw_ref[...], [p0_ref[...], p1_ref[...]])
    o_ref[...] = x + _bdot(p[:, :D_MODEL] * c, wout_ref[...])
    g_ref[...] = g


def _sconv_sample(x, nw, w_in, conv_w, w_out, past):
    sds = jax.ShapeDtypeStruct((DEC_BATCH, D_MODEL), F32)
    x_new, g = _call_whole(_sconv_sample_body, "sconv_sample", [sds, sds],
                           x, nw, w_in, conv_w, w_out, past[:, 0], past[:, 1])
    return x_new, jnp.stack([past[:, 1], g], axis=1)


def _ffn_sample_body(x_ref, nw_ref, wup_ref, cw_ref, cb_ref, wdn_ref, fnw_ref, p0_ref, p1_ref,
                     o_ref, u_ref, *, final):
    x = x_ref[...]
    h = _rms(x, nw_ref[...]).astype(BF16)
    acc = x
    for c in range(D_FF // FFN_CHUNK):
        halves = []
        for base in (0, D_FF):
            lo = base + c * FFN_CHUNK
            hi = lo + FFN_CHUNK
            u = jnp.dot(h, wup_ref[:, lo:hi], preferred_element_type=F32)
            u_ref[:, lo:hi] = u
            halves.append(_conv_step(u, cw_ref[:, lo:hi], [p0_ref[:, lo:hi], p1_ref[:, lo:hi]])
                          + cb_ref[:, lo:hi])
        a, g = halves
        acc = acc + _bdot(_silu(g) * a, wdn_ref[c * FFN_CHUNK:(c + 1) * FFN_CHUNK, :])
    if final:
        acc = _rms(acc, fnw_ref[...])
    o_ref[...] = acc


def _ffn_sample(x, nw, w_up, conv_w, conv_b, w_down, fnw, past, final):
    outs = [jax.ShapeDtypeStruct((DEC_BATCH, D_MODEL), F32),
            jax.ShapeDtypeStruct((DEC_BATCH, 2 * D_FF), F32)]
    x_new, u = _call_whole(functools.partial(_ffn_sample_body, final=final), "ffn_sample", outs,
                           x, nw, w_up, conv_w, conv_b, w_down, fnw, past[:, 0], past[:, 1])
    return x_new, jnp.stack([past[:, 1], u], axis=1)


def _ssd_inproj_sample_body(x_ref, nw_ref, wz_ref, wx_ref, wdt_ref, cw_ref, cb_ref, dtb_ref,
                            p0_ref, p1_ref, p2_ref, z_ref, xbc_ref, dt_ref, u_ref):
    h = _rms(x_ref[...], nw_ref[...]).astype(BF16)
    z_ref[...] = jnp.dot(h, wz_ref[...], preferred_element_type=F32)
    dt_ref[...] = _softplus(jnp.dot(h, wdt_ref[...], preferred_element_type=F32) + dtb_ref[...])
    u = jnp.dot(h, wx_ref[...], preferred_element_type=F32)
    u_ref[...] = u
    conv = _conv_step(u, cw_ref[...], [p0_ref[...], p1_ref[...], p2_ref[...]])
    xbc_ref[...] = _silu(conv + cb_ref[...])


def _ssd_state_sample_body(xbc_ref, dt_ref, alog_ref, dskip_ref, st_ref, y_ref, nst_ref):
    n = SSD_STATE
    xbc = xbc_ref[0]
    dt = dt_ref[0]
    e_dec = jnp.exp(dt * (-jnp.exp(alog_ref[...])))
    row = lax.broadcasted_iota(jnp.int32, (n, n), 0)
    col = lax.broadcasted_iota(jnp.int32, (n, n), 1)
    row_lo = row < SSD_HEAD_DIM
    for j in range(SSD_PAIRS):
        g = j // (SSD_PAIRS // SSD_GROUPS)
        h0 = 2 * j
        b_row = xbc[:, D_INNER + g * n:D_INNER + (g + 1) * n]
        c_row = xbc[:, D_INNER + (SSD_GROUPS + g) * n:D_INNER + (SSD_GROUPS + g + 1) * n]
        xs = xbc[:, j * LANES:(j + 1) * LANES]
        xdt = xs * _pair_lanes(dt, h0, (1, LANES))
        state = st_ref[0, j]
        y = jnp.sum(c_row * b_row, axis=1, keepdims=True) * xdt
        c_rows = jnp.broadcast_to(c_row, (SUBLANES, n))
        y = y + _bdot_nt(c_rows, state)[0:1, :] * _pair_lanes(e_dec, h0, (1, LANES))
        y_ref[0, :, j * LANES:(j + 1) * LANES] = y + xs * dskip_ref[:, j * LANES:(j + 1) * LANES]
        diag = jnp.where(row == col, jnp.broadcast_to(xdt, (n, n)), 0.0)
        upd = _bdot(diag, jnp.broadcast_to(b_row, (n, n)))
        scale = jnp.where(row_lo, e_dec[:, h0:h0 + 1], e_dec[:, h0 + 1:h0 + 2])
        nst_ref[0, j] = state * scale + upd


def _ssd_out_sample_body(x_ref, y_ref, z_ref, nw_ref, wout_ref, o_ref):
    yn = _group_norm_gate(y_ref[...], z_ref[...], nw_ref[...])
    o_ref[...] = x_ref[...] + _bdot(yn, wout_ref[...])


def _ssd_sample(x, nw, w_z, w_x, w_dt, conv_w, conv_b, dt_bias, a_log, d_skip, norm_w, w_out,
                state, conv_past):
    r = DEC_BATCH
    outs = [jax.ShapeDtypeStruct((r, D_INNER), F32), jax.ShapeDtypeStruct((r, SSD_CONV_DIM), F32),
            jax.ShapeDtypeStruct((r, LANES), F32), jax.ShapeDtypeStruct((r, SSD_CONV_DIM), F32)]
    z, xbc, dt, u = _call_whole(_ssd_inproj_sample_body, "ssd_inproj_sample", outs,
                                x, nw, w_z, w_x, w_dt, conv_w, conv_b, dt_bias,
                                conv_past[:, 0], conv_past[:, 1], conv_past[:, 2])
    new_conv = jnp.concatenate([conv_past[:, 1:], u[:, None]], axis=1)
    st_shape = (r, SSD_PAIRS, 2 * SSD_HEAD_DIM, SSD_STATE)
    st_spec = pl.BlockSpec((1,) + st_shape[1:], lambda b: (b, 0, 0, 0))
    vec = lambda w: pl.BlockSpec((1, 1, w), lambda b: (b, 0, 0))
    const = lambda w: pl.BlockSpec((1, w), lambda b: (0, 0))
    y, new_state = pl.pallas_call(
        _ssd_state_sample_body,
        grid=(r,),
        in_specs=[vec(SSD_CONV_DIM), vec(LANES), const(LANES), const(D_INNER), st_spec],
        out_specs=[vec(D_INNER), st_spec],
        out_shape=[jax.ShapeDtypeStruct((r, 1, D_INNER), F32), jax.ShapeDtypeStruct(st_shape, F32)],
        compiler_params=_params("arbitrary"),
        name="ssd_state_sample",
    )(xbc.reshape(r, 1, SSD_CONV_DIM), dt.reshape(r, 1, LANES), a_log, d_skip,
      state.reshape(st_shape))
    (x_new,) = _call_whole(_ssd_out_sample_body, "ssd_out_sample",
                           [jax.ShapeDtypeStruct((r, D_MODEL), F32)],
                           x, y.reshape(r, D_INNER), z, norm_w, w_out)
    return x_new, new_conv, new_state.reshape(r, SSD_HEADS, SSD_HEAD_DIM, SSD_STATE)


def _qkv_sample_body(x_ref, nw_ref, wt_ref, cos_ref, sdn_ref, sup_ref, qt_ref, kt_ref, vt_ref, sown_ref):
    rows = x_ref.shape[0]
    tab = lambda r: jnp.broadcast_to(r[...], (ATTN_HEAD_DIM, rows))
    qt, kt, vt = _qkv_t(x_ref[...], nw_ref[...], wt_ref, tab(cos_ref), tab(sdn_ref), tab(sup_ref))
    qt_ref[...] = qt
    kt_ref[...] = kt
    vt_ref[...] = vt
    per_head = (qt * kt).reshape(ATTN_HEADS, ATTN_HEAD_DIM, rows)
    sown_ref[...] = jnp.sum(per_head, axis=1) * ATTN_SCALE


N_PAST_BLK = PAST_LEN // MOBA_BLOCK
PAGES_PER_BLK = MOBA_BLOCK // PAGE_SIZE
HEAD3 = (ATTN_HEADS, ATTN_HEAD_DIM, LANES)


def _moba_sample_body(pt_ref, qt_ref, vnt_ref, sown_ref, *refs):
    del pt_ref
    k_refs, v_refs = refs[:N_PAGES], refs[N_PAGES:2 * N_PAGES]
    ot_ref, s_ref, p_ref = refs[2 * N_PAGES:]
    b = pl.program_id(0)
    blk = MOBA_BLOCK
    is_seq = lax.broadcasted_iota(jnp.int32, (D_MODEL, LANES), 1) == b
    lane = lax.broadcasted_iota(jnp.int32, (ATTN_HEADS, LANES), 1)

    @pl.when(b == 0)
    def _():
        ot_ref[...] = jnp.zeros_like(ot_ref)

    def column(src_ref):
        col = jnp.sum(jnp.where(is_seq, src_ref[...], 0.0), axis=1, keepdims=True)
        return jnp.broadcast_to(col, (D_MODEL, LANES)).reshape(HEAD3)

    qb = column(qt_ref)
    gate = jnp.zeros((ATTN_HEADS, LANES), F32)
    for j in range(N_PAST_BLK):
        raw_sum = jnp.zeros((ATTN_HEADS, 1), F32)
        for w in range(PAGES_PER_BLK):
            page = j * PAGES_PER_BLK + w
            s = jnp.sum(k_refs[page][0] * qb, axis=1)
            s_ref[:, page * PAGE_SIZE:(page + 1) * PAGE_SIZE] = s * ATTN_SCALE
            raw_sum = raw_sum + jnp.sum(s, axis=1, keepdims=True)
        gate = jnp.where(lane == j, raw_sum / blk, gate)

    s_own = jnp.sum(jnp.where(lane == b, sown_ref[...], 0.0), axis=1, keepdims=True)
    m = s_own
    masked = []
    for j in range(N_PAST_BLK):
        sel = _moba_select(gate, j, N_PAST_BLK)
        sb = jnp.where(sel, s_ref[:, j * blk:(j + 1) * blk], NEG_INF)
        masked.append(sb)
        m = jnp.maximum(m, jnp.max(sb, axis=1, keepdims=True))
    p_own = jnp.exp(s_own - m)
    l = p_own
    ps = []
    for sb in masked:
        pb = jnp.exp(sb - m)
        ps.append(pb)
        l = l + jnp.sum(pb, axis=1, keepdims=True)
    for j in range(N_PAST_BLK):
        p_ref[:, j * blk:(j + 1) * blk] = ps[j] / l

    first_lane = lax.broadcasted_iota(jnp.int32, HEAD3, 2) == 0
    w_own = jnp.broadcast_to(p_own / l, (ATTN_HEADS, LANES)).reshape(ATTN_HEADS, 1, LANES)
    acc = jnp.where(first_lane, w_own * column(vnt_ref), 0.0)
    for page in range(N_PAGES):
        p = p_ref[:, page * PAGE_SIZE:(page + 1) * PAGE_SIZE]
        acc = acc + p.reshape(ATTN_HEADS, 1, PAGE_SIZE) * v_refs[page][0]
    o_col = jnp.sum(acc, axis=2, keepdims=True)
    o_full = jnp.broadcast_to(o_col, HEAD3).reshape(D_MODEL, LANES)
    ot_ref[...] = jnp.where(is_seq, o_full, ot_ref[...])


def _moba_sample(qt, vnt, s_own, cache_kt, cache_vt, page_table):
    r = DEC_BATCH
    whole2 = pl.BlockSpec((D_MODEL, r), lambda b, pt: (0, 0))
    page = lambda w: pl.BlockSpec((1,) + HEAD3, lambda b, pt: (pt[b * N_PAGES + w], 0, 0, 0))
    pages = [page(w) for w in range(N_PAGES)]
    grid_spec = pltpu.PrefetchScalarGridSpec(
        num_scalar_prefetch=1,
        grid=(r,),
        in_specs=[whole2, whole2, pl.BlockSpec((ATTN_HEADS, r), lambda b, pt: (0, 0))] + pages + pages,
        out_specs=whole2,
        scratch_shapes=[
            pltpu.VMEM((ATTN_HEADS, PAST_LEN), F32),
            pltpu.VMEM((ATTN_HEADS, PAST_LEN), F32),
        ],
    )
    return pl.pallas_call(
        _moba_sample_body,
        grid_spec=grid_spec,
        out_shape=jax.ShapeDtypeStruct((D_MODEL, r), F32),
        compiler_params=_params("arbitrary"),
        name="moba_sample",
    )(page_table.reshape(-1), qt, vnt, s_own, *([cache_kt] * N_PAGES), *([cache_vt] * N_PAGES))


def _rope_tables(pos):
    half = ROT_DIM // 2
    inv_freq = ROPE_THETA ** (-(jnp.arange(half, dtype=F32) * 2.0) / ROT_DIM)
    ang = pos.astype(F32)[:, None] * inv_freq
    cos, sin = jnp.cos(ang), jnp.sin(ang)
    ones = jnp.ones((pos.shape[0], ATTN_HEAD_DIM - ROT_DIM), F32)
    zeros = jnp.zeros((pos.shape[0], ATTN_HEAD_DIM - ROT_DIM), F32)
    zh = jnp.zeros_like(sin)
    return (jnp.concatenate([cos, cos, ones], axis=1),
            jnp.concatenate([-sin, zh, zeros], axis=1),
            jnp.concatenate([zh, sin, zeros], axis=1))


def _pad_lanes(a):
    return jnp.pad(a, [(0, 0)] * (a.ndim - 1) + [(0, LANES - a.shape[-1])])


def kernel(x_prompt, x_sample, state_sconv, state_ssm, state_ssm_conv, cache_k, cache_v, state_ffn_conv, page_table, norm_mix_w, norm_ffn_w, norm_final_w, sconv_w_in, sconv_conv_w, sconv_w_out, ssd_w_in, ssd_conv_w, ssd_conv_b, ssd_dt_bias, ssd_a_log, ssd_d, ssd_norm_w, ssd_w_out, attn_w_qkv, attn_w_o, ffn_w_up, ffn_conv_w, ffn_conv_b, ffn_w_down):
    xp = x_prompt.reshape(N_TOK, D_MODEL)
    xs = x_sample.reshape(DEC_BATCH, D_MODEL)
    fnw = norm_final_w.reshape(1, D_MODEL)
    sconv_p, sconv_s, ffnc_p, ffnc_s = [], [], [], []
    for i in range(DEPTH):
        kind, j = i % N_MIXERS, i // N_MIXERS
        nw = norm_mix_w[i].reshape(1, D_MODEL)
        if kind == 0:
            w_in, w_out = sconv_w_in[j].astype(BF16), sconv_w_out[j].astype(BF16)
            xp, st = _sconv_prompt(xp, nw, w_in, sconv_conv_w[j], w_out)
            xs, st_s = _sconv_sample(xs, nw, w_in, sconv_conv_w[j], w_out, state_sconv[j])
            sconv_p.append(st)
            sconv_s.append(st_s)
        elif kind == 1:
            w_in = ssd_w_in[j]
            w_z = w_in[:, :D_INNER].astype(BF16)
            w_x = w_in[:, D_INNER:D_INNER + SSD_CONV_DIM].astype(BF16)
            w_dt = _pad_lanes(w_in[:, D_INNER + SSD_CONV_DIM:]).astype(BF16)
            w_out = ssd_w_out[j].astype(BF16)
            conv_b = ssd_conv_b[j].reshape(1, SSD_CONV_DIM)
            dt_bias = _pad_lanes(ssd_dt_bias[j].reshape(1, SSD_HEADS))
            a_log = _pad_lanes(ssd_a_log[j].reshape(1, SSD_HEADS))
            d_skip = jnp.repeat(ssd_d[j], SSD_HEAD_DIM).reshape(1, D_INNER)
            norm_w = ssd_norm_w[j].reshape(1, D_INNER)
            z, xbc, dt, ssmc_p = _ssd_inproj_prompt(xp, nw, w_z, w_x, w_dt, ssd_conv_w[j], conv_b, dt_bias)
            d_skip_rows = jnp.broadcast_to(d_skip.reshape(D_INNER, 1), (D_INNER, LANES))
            xp, ssm_p = _ssd_scan_prompt(xp, z, xbc, dt, a_log, d_skip_rows, norm_w, w_out)
            ssm_p = ssm_p.reshape(BATCH, SSD_HEADS, SSD_HEAD_DIM, SSD_STATE)
            xs, ssmc_s, ssm_s = _ssd_sample(xs, nw, w_z, w_x, w_dt, ssd_conv_w[j], conv_b, dt_bias,
                                            a_log, d_skip, norm_w, w_out, state_ssm[j], state_ssm_conv[j])
        else:
            w_qkv, w_o = attn_w_qkv[j].astype(BF16), attn_w_o[j].astype(BF16)
            w_qkv_t = w_qkv.T
            tabs = [t.T for t in _rope_tables(jnp.arange(SEQ))]
            qt, kt_p, vt_p, k_rows = _qkv_prompt(xp, nw, w_qkv_t, *tabs)
            xp = _proj_residual_t(_moba_prompt(qt, k_rows, vt_p), w_o, xp, PROJ_TL)
            tabs = [t.T for t in _rope_tables(jnp.full((1,), PAST_LEN))]
            sds = jax.ShapeDtypeStruct((D_MODEL, DEC_BATCH), F32)
            own = jax.ShapeDtypeStruct((ATTN_HEADS, DEC_BATCH), F32)
            qt_s, kt_s, vt_s, s_own = _call_whole(_qkv_sample_body, "qkv_sample", [sds, sds, sds, own],
                                                  xs, nw, w_qkv_t, *tabs)
            pages = lambda c: jnp.transpose(c, (0, 2, 3, 1))
            ot_s = _moba_sample(qt_s, vt_s, s_own, pages(cache_k[j]), pages(cache_v[j]), page_table)
            xs = _proj_residual_t(ot_s[None], w_o, xs, DEC_BATCH)
        nwf = norm_ffn_w[i].reshape(1, D_MODEL)
        w_up, w_down = ffn_w_up[i].astype(BF16), ffn_w_down[i].astype(BF16)
        conv_b = ffn_conv_b[i].reshape(1, 2 * D_FF)
        final = i == DEPTH - 1
        xp, fc_p = _ffn_prompt(xp, nwf, w_up, ffn_conv_w[i], conv_b, w_down, fnw, final)
        xs, fc_s = _ffn_sample(xs, nwf, w_up, ffn_conv_w[i], conv_b, w_down, fnw, state_ffn_conv[i], final)
        ffnc_p.append(fc_p)
        ffnc_s.append(fc_s)
    heads = (ATTN_HEADS, ATTN_HEAD_DIM)
    kv_p = lambda t: jnp.transpose(t.reshape((BATCH,) + heads + (SEQ,)), (0, 3, 1, 2))[None]
    kv_s = lambda t: jnp.transpose(t.reshape(heads + (DEC_BATCH,)), (2, 0, 1))[None, :, None]
    return (xp.reshape(BATCH, SEQ, D_MODEL), xs.reshape(DEC_BATCH, 1, D_MODEL),
            jnp.stack(sconv_p), jnp.stack(sconv_s),
            ssm_p[None], ssm_s[None], ssmc_p[None], ssmc_s[None],
            kv_p(kt_p), kv_p(vt_p), kv_s(kt_s), kv_s(vt_s),
            jnp.stack(ffnc_p), jnp.stack(ffnc_s))
```

```python
import functools

import jax
import jax.numpy as jnp
from jax import lax
from jax.experimental import pallas as pl
from jax.experimental.pallas import tpu as pltpu

F32 = jnp.float32
BF16 = jnp.bfloat16

D_MODEL = 1024
BATCH = 8
SEQ = 2048
DEPTH = 4
DEC_BATCH = 128
PAST_LEN = 2048
PAGE_SIZE = 128
N_PAGES = PAST_LEN // PAGE_SIZE
N_MIXERS = 3
RMS_EPS = 1e-6
SCONV_WIDTH = 3
D_INNER = 2 * D_MODEL
SSD_HEAD_DIM = 64
SSD_HEADS = D_INNER // SSD_HEAD_DIM
SSD_GROUPS = 4
SSD_STATE = 128
SSD_CONV_WIDTH = 4
SSD_CONV_DIM = D_INNER + 2 * SSD_GROUPS * SSD_STATE
SSD_CHUNK = 128
SSD_PAIRS = SSD_HEADS // 2
ATTN_HEAD_DIM = 64
ATTN_HEADS = D_MODEL // ATTN_HEAD_DIM
MOBA_BLOCK = 256
MOBA_TOP_K = 3
ROPE_THETA = 500000.0
ROT_DIM = ATTN_HEAD_DIM // 4
D_FF = 2816
FFN_CONV_WIDTH = 3

LANES = 128
SUBLANES = 8
VMEM_LIMIT = 56 * 1024 * 1024
N_TOK = BATCH * SEQ
NEG_INF = float("-inf")


def _params(*sem):
    return pltpu.CompilerParams(dimension_semantics=sem, vmem_limit_bytes=VMEM_LIMIT)


def _bdot(a, b):
    return jnp.dot(a.astype(BF16), b.astype(BF16), preferred_element_type=F32)


def _bdot_nt(a, b):
    return lax.dot_general(a.astype(BF16), b.astype(BF16), (((1,), (1,)), ((), ())),
                           preferred_element_type=F32)


def _rms(x, w):
    inv = lax.rsqrt(jnp.mean(x * x, axis=-1, keepdims=True) + RMS_EPS)
    return (x * inv) * w


def _silu(x):
    return x * (1.0 / (1.0 + jnp.exp(-x)))


def _softplus(x):
    return jnp.maximum(x, 0.0) + jnp.log1p(jnp.exp(-jnp.abs(x)))


def _conv_rows(u, w, prev):
    width = w.shape[0]
    row8 = lax.broadcasted_iota(jnp.int32, (SUBLANES, u.shape[1]), 0)
    out = u * w[width - 1:width, :]
    for k in range(1, width):
        sh = pltpu.roll(u, k, axis=0)
        head = sh[:SUBLANES]
        for t in range(k):
            src = SUBLANES - k + t
            head = jnp.where(row8 == t, prev[src:src + 1, :], head)
        sh = jnp.concatenate([head, sh[SUBLANES:]], axis=0)
        out = out + sh * w[width - 1 - k:width - k, :]
    return out


def _conv_step(u, w, past):
    width = w.shape[0]
    out = u * w[width - 1:width, :]
    for k in range(width - 1):
        out = out + past[k] * w[k:k + 1, :]
    return out


SCONV_TL = 1024


def _sconv_prompt_body(x_ref, nw_ref, win_ref, cw_ref, wout_ref, o_ref, st_ref, carry_ref):
    l = pl.program_id(1)

    @pl.when(l == 0)
    def _():
        carry_ref[...] = jnp.zeros_like(carry_ref)

    x = x_ref[...]
    h = _rms(x, nw_ref[...]).astype(BF16)
    p = jnp.dot(h, win_ref[...], preferred_element_type=F32)
    g = p[:, D_MODEL:2 * D_MODEL] * p[:, 2 * D_MODEL:]
    c = _conv_rows(g, cw_ref[...], carry_ref[...])
    carry_ref[...] = g[SCONV_TL - SUBLANES:, :]
    o_ref[...] = x + _bdot(p[:, :D_MODEL] * c, wout_ref[...])

    @pl.when(l == pl.num_programs(1) - 1)
    def _():
        st_ref[0] = g[SCONV_TL - (SCONV_WIDTH - 1):, :]


def _sconv_prompt(x, nw, w_in, conv_w, w_out):
    nl = SEQ // SCONV_TL
    row = lambda b, l: (b * nl + l, 0)
    const = lambda b, l: (0, 0)
    return pl.pallas_call(
        _sconv_prompt_body,
        grid=(BATCH, nl),
        in_specs=[
            pl.BlockSpec((SCONV_TL, D_MODEL), row),
            pl.BlockSpec((1, D_MODEL), const),
            pl.BlockSpec((D_MODEL, 3 * D_MODEL), const),
            pl.BlockSpec((SCONV_WIDTH, D_MODEL), const),
            pl.BlockSpec((D_MODEL, D_MODEL), const),
        ],
        out_specs=[
            pl.BlockSpec((SCONV_TL, D_MODEL), row),
            pl.BlockSpec((1, SCONV_WIDTH - 1, D_MODEL), lambda b, l: (b, 0, 0)),
        ],
        out_shape=[
            jax.ShapeDtypeStruct((N_TOK, D_MODEL), F32),
            jax.ShapeDtypeStruct((BATCH, SCONV_WIDTH - 1, D_MODEL), F32),
        ],
        scratch_shapes=[pltpu.VMEM((SUBLANES, D_MODEL), F32)],
        compiler_params=_params("arbitrary", "arbitrary"),
        name="sconv_prompt",
    )(x, nw, w_in, conv_w, w_out)


FFN_TL = 1024
FFN_CHUNK = D_FF // 2


def _ffn_prompt_body(x_ref, nw_ref, wup_ref, cw_ref, cb_ref, wdn_ref, fnw_ref, o_ref, st_ref,
                     carry_ref, *, final):
    l = pl.program_id(1)

    @pl.when(l == 0)
    def _():
        carry_ref[...] = jnp.zeros_like(carry_ref)

    x = x_ref[...]
    h = _rms(x, nw_ref[...]).astype(BF16)
    acc = x
    last = l == pl.num_programs(1) - 1
    for c in range(D_FF // FFN_CHUNK):
        halves = []
        for base in (0, D_FF):
            lo = base + c * FFN_CHUNK
            hi = lo + FFN_CHUNK
            u = jnp.dot(h, wup_ref[:, lo:hi], preferred_element_type=F32)
            conv = _conv_rows(u, cw_ref[:, lo:hi], carry_ref[:, lo:hi]) + cb_ref[:, lo:hi]
            carry_ref[:, lo:hi] = u[FFN_TL - SUBLANES:, :]

            @pl.when(last)
            def _(u=u, lo=lo, hi=hi):
                st_ref[0, :, lo:hi] = u[FFN_TL - (FFN_CONV_WIDTH - 1):, :]

            halves.append(conv)
        a, g = halves
        acc = acc + _bdot(_silu(g) * a, wdn_ref[c * FFN_CHUNK:(c + 1) * FFN_CHUNK, :])
    if final:
        acc = _rms(acc, fnw_ref[...])
    o_ref[...] = acc


def _ffn_prompt(x, nw, w_up, conv_w, conv_b, w_down, fnw, final):
    nl = SEQ // FFN_TL
    row = lambda b, l: (b * nl + l, 0)
    const = lambda b, l: (0, 0)
    once = pl.Buffered(1)
    return pl.pallas_call(
        functools.partial(_ffn_prompt_body, final=final),
        grid=(BATCH, nl),
        in_specs=[
            pl.BlockSpec((FFN_TL, D_MODEL), row),
            pl.BlockSpec((1, D_MODEL), const),
            pl.BlockSpec((D_MODEL, 2 * D_FF), const, pipeline_mode=once),
            pl.BlockSpec((FFN_CONV_WIDTH, 2 * D_FF), const),
            pl.BlockSpec((1, 2 * D_FF), const),
            pl.BlockSpec((D_FF, D_MODEL), const, pipeline_mode=once),
            pl.BlockSpec((1, D_MODEL), const),
        ],
        out_specs=[
            pl.BlockSpec((FFN_TL, D_MODEL), row),
            pl.BlockSpec((1, FFN_CONV_WIDTH - 1, 2 * D_FF), lambda b, l: (b, 0, 0)),
        ],
        out_shape=[
            jax.ShapeDtypeStruct((N_TOK, D_MODEL), F32),
            jax.ShapeDtypeStruct((BATCH, FFN_CONV_WIDTH - 1, 2 * D_FF), F32),
        ],
        scratch_shapes=[pltpu.VMEM((SUBLANES, 2 * D_FF), F32)],
        compiler_params=_params("arbitrary", "arbitrary"),
        name="ffn_prompt",
    )(x, nw, w_up, conv_w, conv_b, w_down, fnw)


SSD_TL = 256


def _ssd_inproj_prompt_body(x_ref, nw_ref, wz_ref, wx_ref, wdt_ref, cw_ref, cb_ref, dtb_ref,
                            z_ref, xbc_ref, dt_ref, st_ref, carry_ref):
    l = pl.program_id(1)

    @pl.when(l == 0)
    def _():
        carry_ref[...] = jnp.zeros_like(carry_ref)

    h = _rms(x_ref[...], nw_ref[...]).astype(BF16)
    z_ref[...] = jnp.dot(h, wz_ref[...], preferred_element_type=F32)
    dt_ref[...] = _softplus(jnp.dot(h, wdt_ref[...], preferred_element_type=F32) + dtb_ref[...])
    u = jnp.dot(h, wx_ref[...], preferred_element_type=F32)
    xbc_ref[...] = _silu(_conv_rows(u, cw_ref[...], carry_ref[...]) + cb_ref[...])
    carry_ref[...] = u[SSD_TL - SUBLANES:, :]

    @pl.when(l == pl.num_programs(1) - 1)
    def _():
        st_ref[0] = u[SSD_TL - (SSD_CONV_WIDTH - 1):, :]


def _ssd_inproj_prompt(x, nw, w_z, w_x, w_dt, conv_w, conv_b, dt_bias):
    nl = SEQ // SSD_TL
    row = lambda b, l: (b * nl + l, 0)
    const = lambda b, l: (0, 0)
    return pl.pallas_call(
        _ssd_inproj_prompt_body,
        grid=(BATCH, nl),
        in_specs=[
            pl.BlockSpec((SSD_TL, D_MODEL), row),
            pl.BlockSpec((1, D_MODEL), const),
            pl.BlockSpec((D_MODEL, D_INNER), const),
            pl.BlockSpec((D_MODEL, SSD_CONV_DIM), const),
            pl.BlockSpec((D_MODEL, LANES), const),
            pl.BlockSpec((SSD_CONV_WIDTH, SSD_CONV_DIM), const),
            pl.BlockSpec((1, SSD_CONV_DIM), const),
            pl.BlockSpec((1, LANES), const),
        ],
        out_specs=[
            pl.BlockSpec((SSD_TL, D_INNER), row),
            pl.BlockSpec((SSD_TL, SSD_CONV_DIM), row),
            pl.BlockSpec((SSD_TL, LANES), row),
            pl.BlockSpec((1, SSD_CONV_WIDTH - 1, SSD_CONV_DIM), lambda b, l: (b, 0, 0)),
        ],
        out_shape=[
            jax.ShapeDtypeStruct((N_TOK, D_INNER), F32),
            jax.ShapeDtypeStruct((N_TOK, SSD_CONV_DIM), F32),
            jax.ShapeDtypeStruct((N_TOK, LANES), F32),
            jax.ShapeDtypeStruct((BATCH, SSD_CONV_WIDTH - 1, SSD_CONV_DIM), F32),
        ],
        scratch_shapes=[pltpu.VMEM((SUBLANES, SSD_CONV_DIM), F32)],
        compiler_params=_params("arbitrary", "arbitrary"),
        name="ssd_inproj_prompt",
    )(x, nw, w_z, w_x, w_dt, conv_w, conv_b, dt_bias)


def _cumsum_rows(x):
    n = x.shape[0]
    row = lax.broadcasted_iota(jnp.int32, x.shape, 0)
    k = 1
    while k < n:
        x = x + jnp.where(row >= k, pltpu.roll(x, k, axis=0), 0.0)
        k *= 2
    return x


def _pair_lanes(vals, h0, shape):
    lane = lax.broadcasted_iota(jnp.int32, shape, 1)
    return jnp.where(lane < SSD_HEAD_DIM, vals[:, h0:h0 + 1], vals[:, h0 + 1:h0 + 2])


def _group_norm_gate(y, z, nw):
    y = y * _silu(z)
    gw = D_INNER // SSD_GROUPS
    outs = []
    for g in range(SSD_GROUPS):
        yg = y[:, g * gw:(g + 1) * gw]
        outs.append(_rms(yg, nw[:, g * gw:(g + 1) * gw]))
    return jnp.concatenate(outs, axis=1)


def _ssd_scan_prompt_body(x_ref, z_ref, xbc_ref, dt_ref, alog_ref, dskip_ref, nw_ref, wout_ref,
                          o_ref, st_ref, y_ref):
    c = pl.program_id(1)

    @pl.when(c == 0)
    def _():
        st_ref[...] = jnp.zeros_like(st_ref)

    q = SSD_CHUNK
    p2 = 2 * SSD_HEAD_DIM
    dt = dt_ref[...]
    acum = _cumsum_rows(dt * (-jnp.exp(alog_ref[...])))
    last = acum[q - 1:q, :]
    acum_t = acum.T
    dt_t = dt.T
    w_t = (dt * jnp.exp(last - acum)).T
    e_acum_t = jnp.exp(acum_t)
    e_chunk = jnp.exp(last)
    row = lax.broadcasted_iota(jnp.int32, (q, q), 0)
    col = lax.broadcasted_iota(jnp.int32, (q, q), 1)
    causal_t = col >= row
    gp = SSD_PAIRS // SSD_GROUPS
    gw = gp * p2
    head_of_row = lax.broadcasted_iota(jnp.int32, (gw, q), 0) // SSD_HEAD_DIM
    head_of_state = lax.broadcasted_iota(jnp.int32, (gw, SSD_STATE), 0) // SSD_HEAD_DIM

    def group_rows(vals_t, g):
        h_first = g * 2 * gp
        out = jnp.broadcast_to(vals_t[h_first:h_first + 1, :], (gw, q))
        for e in range(1, 2 * gp):
            out = jnp.where(head_of_row >= e, vals_t[h_first + e:h_first + e + 1, :], out)
        return out

    for g in range(SSD_GROUPS):
        b_g = xbc_ref[:, D_INNER + g * SSD_STATE:D_INNER + (g + 1) * SSD_STATE].astype(BF16)
        c_g = xbc_ref[:, D_INNER + (SSD_GROUPS + g) * SSD_STATE:D_INNER + (SSD_GROUPS + g + 1) * SSD_STATE]
        cb_t = _bdot_nt(b_g, c_g)
        c_t = c_g.T.astype(BF16)
        xs_t = xbc_ref[:, g * gw:(g + 1) * gw].T
        xdt_t = (xs_t * group_rows(dt_t, g)).astype(BF16)
        yd = []
        for e in range(2 * gp):
            h = g * 2 * gp + e
            seg = acum_t[h:h + 1, :] - acum[:, h:h + 1]
            decay = jnp.exp(jnp.where(causal_t, seg, NEG_INF))
            yd.append(_bdot(xdt_t[e * SSD_HEAD_DIM:(e + 1) * SSD_HEAD_DIM, :], cb_t * decay))
        state = st_ref[0, g * gp:(g + 1) * gp].reshape(gw, SSD_STATE)
        y_t = jnp.concatenate(yd, axis=0) + _bdot(state, c_t) * group_rows(e_acum_t, g)
        y_t = y_t + xs_t * dskip_ref[g * gw:(g + 1) * gw, :]
        y_ref[:, g * gw:(g + 1) * gw] = y_t.T
        upd = _bdot(xs_t * group_rows(w_t, g), b_g)
        scale = jnp.broadcast_to(e_chunk[:, g * 2 * gp:g * 2 * gp + 1], (gw, SSD_STATE))
        for e in range(1, 2 * gp):
            scale = jnp.where(head_of_state >= e, e_chunk[:, g * 2 * gp + e:g * 2 * gp + e + 1], scale)
        st_ref[0, g * gp:(g + 1) * gp] = (state * scale + upd).reshape(gp, p2, SSD_STATE)

    yn = _group_norm_gate(y_ref[...], z_ref[...], nw_ref[...])
    o_ref[...] = x_ref[...] + _bdot(yn, wout_ref[...])


def _ssd_scan_prompt(x, z, xbc, dt, a_log, d_skip, nw, w_out):
    nc = SEQ // SSD_CHUNK
    row = lambda b, c: (b * nc + c, 0)
    const = lambda b, c: (0, 0)
    return pl.pallas_call(
        _ssd_scan_prompt_body,
        grid=(BATCH, nc),
        in_specs=[
            pl.BlockSpec((SSD_CHUNK, D_MODEL), row),
            pl.BlockSpec((SSD_CHUNK, D_INNER), row),
            pl.BlockSpec((SSD_CHUNK, SSD_CONV_DIM), row),
            pl.BlockSpec((SSD_CHUNK, LANES), row),
            pl.BlockSpec((1, LANES), const),
            pl.BlockSpec((D_INNER, LANES), const),
            pl.BlockSpec((1, D_INNER), const),
            pl.BlockSpec((D_INNER, D_MODEL), const),
        ],
        out_specs=[
            pl.BlockSpec((SSD_CHUNK, D_MODEL), row),
            pl.BlockSpec((1, SSD_PAIRS, 2 * SSD_HEAD_DIM, SSD_STATE), lambda b, c: (b, 0, 0, 0)),
        ],
        out_shape=[
            jax.ShapeDtypeStruct((N_TOK, D_MODEL), F32),
            jax.ShapeDtypeStruct((BATCH, SSD_PAIRS, 2 * SSD_HEAD_DIM, SSD_STATE), F32),
        ],
        scratch_shapes=[pltpu.VMEM((SSD_CHUNK, D_INNER), F32)],
        compiler_params=_params("arbitrary", "arbitrary"),
        name="ssd_scan_prompt",
    )(x, z, xbc, dt, a_log, d_skip, nw, w_out)


QKV_TL = 512


def _rope_t(t, cos, sin_dn, sin_up):
    reps = t.shape[0] // ATTN_HEAD_DIM
    half = ROT_DIM // 2
    tile = lambda a: jnp.concatenate([a] * reps, axis=0)
    return (t * tile(cos)
            + pltpu.roll(t, t.shape[0] - half, axis=0) * tile(sin_dn)
            + pltpu.roll(t, half, axis=0) * tile(sin_up))


def _qkv_t(x, nw, wt_ref, cos, sdn, sup):
    ht = _rms(x, nw).T.astype(BF16)
    qt = _rope_t(jnp.dot(wt_ref[:D_MODEL, :], ht, preferred_element_type=F32), cos, sdn, sup)
    kt = _rope_t(jnp.dot(wt_ref[D_MODEL:2 * D_MODEL, :], ht, preferred_element_type=F32), cos, sdn, sup)
    vt = jnp.dot(wt_ref[2 * D_MODEL:, :], ht, preferred_element_type=F32)
    return qt, kt, vt


def _qkv_prompt_body(x_ref, nw_ref, wt_ref, cos_ref, sdn_ref, sup_ref, qt_ref, kt_ref, vt_ref, k_ref):
    qt, kt, vt = _qkv_t(x_ref[...], nw_ref[...], wt_ref, cos_ref[...], sdn_ref[...], sup_ref[...])
    qt_ref[0] = qt
    kt_ref[0] = kt
    vt_ref[0] = vt
    k_ref[...] = kt.T


def _qkv_prompt(x, nw, w_qkv_t, cos, sdn, sup):
    nl = SEQ // QKV_TL
    row = lambda b, l: (b * nl + l, 0)
    const = lambda b, l: (0, 0)
    tab = pl.BlockSpec((ATTN_HEAD_DIM, QKV_TL), lambda b, l: (0, l))
    tspec = pl.BlockSpec((1, D_MODEL, QKV_TL), lambda b, l: (b, 0, l))
    tout = jax.ShapeDtypeStruct((BATCH, D_MODEL, SEQ), F32)
    return pl.pallas_call(
        _qkv_prompt_body,
        grid=(BATCH, nl),
        in_specs=[
            pl.BlockSpec((QKV_TL, D_MODEL), row),
            pl.BlockSpec((1, D_MODEL), const),
            pl.BlockSpec((3 * D_MODEL, D_MODEL), const),
            tab, tab, tab,
        ],
        out_specs=[tspec, tspec, tspec, pl.BlockSpec((QKV_TL, D_MODEL), row)],
        out_shape=[tout, tout, tout, jax.ShapeDtypeStruct((N_TOK, D_MODEL), F32)],
        compiler_params=_params("arbitrary", "arbitrary"),
        name="qkv_prompt",
    )(x, nw, w_qkv_t, cos, sdn, sup)


N_BLK = SEQ // MOBA_BLOCK
ATTN_SCALE = ATTN_HEAD_DIM ** -0.5
LOG2_E = 1.4426950408889634


def _moba_select(gate, n, own):
    ax = gate.ndim - 1
    lane = lax.broadcasted_iota(jnp.int32, gate.shape, ax)
    g_n = jnp.sum(jnp.where(lane == n, gate, 0.0), axis=ax, keepdims=True)
    ahead = (gate > g_n) | ((gate == g_n) & (lane < n))
    rank = jnp.sum(jnp.where(ahead & (lane < own), 1.0, 0.0), axis=ax, keepdims=True)
    return rank < MOBA_TOP_K


def _moba_bias_rows(gate_t, own):
    ax = gate_t.ndim - 2
    row = lax.broadcasted_iota(jnp.int32, gate_t.shape, ax)
    past = row < own
    bias = jnp.full(gate_t.shape, NEG_INF, F32)
    for n in range(gate_t.shape[ax] - 1):
        g_n = gate_t[:, n:n + 1, :]
        ahead = (gate_t > g_n) | ((gate_t == g_n) & (row < n))
        rank = jnp.sum(jnp.where(ahead & past, 1.0, 0.0), axis=ax, keepdims=True)
        chosen = (rank < MOBA_TOP_K) & (row == n) & past
        bias = jnp.where(chosen, 0.0, bias)
    return bias


def _dot3(a, b):
    a_hi = a.astype(BF16)
    b_hi = b.astype(BF16)
    a_lo = (a - a_hi.astype(F32)).astype(BF16)
    b_lo = (b - b_hi.astype(F32)).astype(BF16)
    return (jnp.dot(a_hi, b_hi, preferred_element_type=F32)
            + jnp.dot(a_hi, b_lo, preferred_element_type=F32)
            + jnp.dot(a_lo, b_hi, preferred_element_type=F32))


MOBA_HEADS = 4
MOBA_ROWS = MOBA_HEADS * ATTN_HEAD_DIM


def _moba_prompt_body(qt_ref, k_ref, vt_ref, ot_ref, kmean_ref, bias_ref, sa_ref, sb_ref):
    i = pl.program_id(2)
    blk = MOBA_BLOCK
    dh = ATTN_HEAD_DIM
    heads = range(MOBA_HEADS)

    @pl.when(i == 0)
    def _():
        for n in range(N_BLK):
            kmean_ref[n:n + 1, :] = jnp.mean(k_ref[n * blk:(n + 1) * blk, :], axis=0, keepdims=True)

    qt = qt_ref[0]
    q_row = lax.broadcasted_iota(jnp.int32, qt.shape, 0)
    km = kmean_ref[...]
    km_lane = lax.broadcasted_iota(jnp.int32, km.shape, 1)
    qs = qt * (ATTN_SCALE * LOG2_E)
    q_w = [jnp.where((q_row >= hh * dh) & (q_row < (hh + 1) * dh), qs, 0.0).astype(BF16) for hh in heads]
    km_heads = [jnp.where((km_lane >= hh * dh) & (km_lane < (hh + 1) * dh), km, 0.0) for hh in heads]
    gates = _dot3(jnp.concatenate(km_heads, axis=0), qt)
    bias_ref[...] = _moba_bias_rows(gates.reshape(MOBA_HEADS, N_BLK, blk), i)

    key = lax.broadcasted_iota(jnp.int32, (blk, blk), 0)
    qry = lax.broadcasted_iota(jnp.int32, (blk, blk), 1)

    def scores(n, dst_ref):
        k_n = k_ref[pl.ds(pl.multiple_of(n * blk, blk), blk), :].astype(BF16)
        for hh in heads:
            dst_ref[hh] = jnp.dot(k_n, q_w[hh], preferred_element_type=F32)

    def absorb(n, masked, state):
        vt_n = vt_ref[0, :, pl.ds(pl.multiple_of(n * blk, blk), blk)].astype(BF16)
        out = []
        for hh in heads:
            m, l, acc = state[3 * hh:3 * hh + 3]
            s = masked(hh)
            m_new = jnp.maximum(m, jnp.max(s, axis=0, keepdims=True))
            m_ref = jnp.where(m_new == NEG_INF, 0.0, m_new)
            alpha = jnp.exp2(m - m_ref)
            p = jnp.exp2(s - m_ref)
            l = l * alpha + jnp.sum(p, axis=0, keepdims=True)
            acc = acc * alpha + jnp.dot(vt_n[hh * dh:(hh + 1) * dh, :], p.astype(BF16),
                                        preferred_element_type=F32)
            out += [m_new, l, acc]
        return tuple(out)

    def past(n, src_ref):
        return lambda hh: src_ref[hh] + bias_ref[hh, pl.ds(n, 1), :]

    def block_pair(p, state):
        scores(jnp.minimum(2 * p + 1, i), sb_ref)
        state = absorb(2 * p, past(2 * p, sa_ref), state)
        scores(jnp.minimum(2 * p + 2, i), sa_ref)
        return absorb(jnp.minimum(2 * p + 1, i), past(2 * p + 1, sb_ref), state)

    scores(0, sa_ref)
    init = (jnp.full((1, blk), NEG_INF, F32), jnp.zeros((1, blk), F32), jnp.zeros((dh, blk), F32))
    state = lax.fori_loop(0, (i + 1) // 2, block_pair, init * MOBA_HEADS)
    final = absorb(i, lambda hh: jnp.where(key <= qry, sa_ref[hh], NEG_INF), state)
    ot_ref[0] = jnp.concatenate([final[3 * hh + 2] / final[3 * hh + 1] for hh in heads], axis=0)


def _moba_prompt(qt, k, vt):
    groups = ATTN_HEADS // MOBA_HEADS
    qspec = pl.BlockSpec((1, MOBA_ROWS, MOBA_BLOCK), lambda b, p, i: (b, p, i))
    return pl.pallas_call(
        _moba_prompt_body,
        grid=(BATCH, groups, N_BLK),
        in_specs=[
            qspec,
            pl.BlockSpec((SEQ, MOBA_ROWS), lambda b, p, i: (b, p)),
            pl.BlockSpec((1, MOBA_ROWS, SEQ), lambda b, p, i: (b, p, 0)),
        ],
        out_specs=qspec,
        out_shape=jax.ShapeDtypeStruct((BATCH, D_MODEL, SEQ), F32),
        scratch_shapes=[pltpu.VMEM((N_BLK, MOBA_ROWS), F32),
                        pltpu.VMEM((MOBA_HEADS, N_BLK, MOBA_BLOCK), F32),
                        pltpu.VMEM((MOBA_HEADS, MOBA_BLOCK, MOBA_BLOCK), F32),
                        pltpu.VMEM((MOBA_HEADS, MOBA_BLOCK, MOBA_BLOCK), F32)],
        compiler_params=_params("arbitrary", "arbitrary", "arbitrary"),
        name="moba_prompt",
    )(qt, k, vt)


PROJ_TL = 512


def _proj_residual_t_body(at_ref, w_ref, x_ref, o_ref):
    o_ref[...] = x_ref[...] + _bdot(at_ref[0].T, w_ref[...])


def _proj_residual_t(at, w, x, tl):
    groups, kdim, per = at.shape
    nl = per // tl
    row = lambda g, l: (g * nl + l, 0)
    return pl.pallas_call(
        _proj_residual_t_body,
        grid=(groups, nl),
        in_specs=[
            pl.BlockSpec((1, kdim, tl), lambda g, l: (g, 0, l)),
            pl.BlockSpec(w.shape, lambda g, l: (0, 0)),
            pl.BlockSpec((tl, w.shape[1]), row),
        ],
        out_specs=pl.BlockSpec((tl, w.shape[1]), row),
        out_shape=jax.ShapeDtypeStruct(x.shape, F32),
        compiler_params=_params("arbitrary", "arbitrary"),
        name="proj_residual",
    )(at, w, x)


def _whole(shape):
    return pl.BlockSpec(shape, lambda: (0,) * len(shape))


def _call_whole(body, name, out_shapes, *args):
    return pl.pallas_call(
        body,
        in_specs=[_whole(a.shape) for a in args],
        out_specs=[_whole(s.shape) for s in out_shapes],
        out_shape=out_shapes,
        compiler_params=pltpu.CompilerParams(vmem_limit_bytes=VMEM_LIMIT),
        name=name,
    )(*args)


def _sconv_sample_body(x_ref, nw_ref, win_ref, cw_ref, wout_ref, p0_ref, p1_ref, o_ref, g_ref):
    x = x_ref[...]
    h = _rms(x, nw_ref[...]).astype(BF16)
    p = jnp.dot(h, win_ref[...], preferred_element_type=F32)
    g = p[:, D_MODEL:2 * D_MODEL] * p[:, 2 * D_MODEL:]
    c = _conv_step(g, cw_ref[...], [p0_ref[...], p1_ref[...]])
    o_ref[...] = x + _bdot(p[:, :D_MODEL] * c, wout_ref[...])
    g_ref[...] = g


def _sconv_sample(x, nw, w_in, conv_w, w_out, past):
    sds = jax.ShapeDtypeStruct((DEC_BATCH, D_MODEL), F32)
    x_new, g = _call_whole(_sconv_sample_body, "sconv_sample", [sds, sds],
                           x, nw, w_in, conv_w, w_out, past[:, 0], past[:, 1])
    return x_new, jnp.stack([past[:, 1], g], axis=1)


def _ffn_sample_body(x_ref, nw_ref, wup_ref, cw_ref, cb_ref, wdn_ref, fnw_ref, p0_ref, p1_ref,
                     o_ref, u_ref, *, final):
    x = x_ref[...]
    h = _rms(x, nw_ref[...]).astype(BF16)
    acc = x
    for c in range(D_FF // FFN_CHUNK):
        halves = []
        for base in (0, D_FF):
            lo = base + c * FFN_CHUNK
            hi = lo + FFN_CHUNK
            u = jnp.dot(h, wup_ref[:, lo:hi], preferred_element_type=F32)
            u_ref[:, lo:hi] = u
            halves.append(_conv_step(u, cw_ref[:, lo:hi], [p0_ref[:, lo:hi], p1_ref[:, lo:hi]])
                          + cb_ref[:, lo:hi])
        a, g = halves
        acc = acc + _bdot(_silu(g) * a, wdn_ref[c * FFN_CHUNK:(c + 1) * FFN_CHUNK, :])
    if final:
        acc = _rms(acc, fnw_ref[...])
    o_ref[...] = acc


def _ffn_sample(x, nw, w_up, conv_w, conv_b, w_down, fnw, past, final):
    outs = [jax.ShapeDtypeStruct((DEC_BATCH, D_MODEL), F32),
            jax.ShapeDtypeStruct((DEC_BATCH, 2 * D_FF), F32)]
    x_new, u = _call_whole(functools.partial(_ffn_sample_body, final=final), "ffn_sample", outs,
                           x, nw, w_up, conv_w, conv_b, w_down, fnw, past[:, 0], past[:, 1])
    return x_new, jnp.stack([past[:, 1], u], axis=1)


def _ssd_inproj_sample_body(x_ref, nw_ref, wz_ref, wx_ref, wdt_ref, cw_ref, cb_ref, dtb_ref,
                            p0_ref, p1_ref, p2_ref, z_ref, xbc_ref, dt_ref, u_ref):
    h = _rms(x_ref[...], nw_ref[...]).astype(BF16)
    z_ref[...] = jnp.dot(h, wz_ref[...], preferred_element_type=F32)
    dt_ref[...] = _softplus(jnp.dot(h, wdt_ref[...], preferred_element_type=F32) + dtb_ref[...])
    u = jnp.dot(h, wx_ref[...], preferred_element_type=F32)
    u_ref[...] = u
    conv = _conv_step(u, cw_ref[...], [p0_ref[...], p1_ref[...], p2_ref[...]])
    xbc_ref[...] = _silu(conv + cb_ref[...])


STATE_SEQS = 4


def _ssd_state_sample_body(xbc_ref, dt_ref, alog_ref, dskip_ref, st_ref, y_ref, nst_ref):
    n = SSD_STATE
    row = lax.broadcasted_iota(jnp.int32, (n, n), 0)
    col = lax.broadcasted_iota(jnp.int32, (n, n), 1)
    row_lo = row < SSD_HEAD_DIM
    for t in range(STATE_SEQS):
        xbc = xbc_ref[t]
        dt = dt_ref[t]
        e_dec = jnp.exp(dt * (-jnp.exp(alog_ref[...])))
        for j in range(SSD_PAIRS):
            g = j // (SSD_PAIRS // SSD_GROUPS)
            h0 = 2 * j
            b_row = xbc[:, D_INNER + g * n:D_INNER + (g + 1) * n]
            c_row = xbc[:, D_INNER + (SSD_GROUPS + g) * n:D_INNER + (SSD_GROUPS + g + 1) * n]
            xs = xbc[:, j * LANES:(j + 1) * LANES]
            xdt = xs * _pair_lanes(dt, h0, (1, LANES))
            state = st_ref[t, j]
            y = jnp.sum(c_row * b_row, axis=1, keepdims=True) * xdt
            c_rows = jnp.broadcast_to(c_row, (SUBLANES, n))
            y = y + _bdot_nt(c_rows, state)[0:1, :] * _pair_lanes(e_dec, h0, (1, LANES))
            y_ref[t, :, j * LANES:(j + 1) * LANES] = y + xs * dskip_ref[:, j * LANES:(j + 1) * LANES]
            diag = jnp.where(row == col, jnp.broadcast_to(xdt, (n, n)), 0.0)
            upd = _bdot(diag, jnp.broadcast_to(b_row, (n, n)))
            scale = jnp.where(row_lo, e_dec[:, h0:h0 + 1], e_dec[:, h0 + 1:h0 + 2])
            nst_ref[t, j] = state * scale + upd


def _ssd_out_sample_body(x_ref, y_ref, z_ref, nw_ref, wout_ref, o_ref):
    yn = _group_norm_gate(y_ref[...], z_ref[...], nw_ref[...])
    o_ref[...] = x_ref[...] + _bdot(yn, wout_ref[...])


def _ssd_sample(x, nw, w_z, w_x, w_dt, conv_w, conv_b, dt_bias, a_log, d_skip, norm_w, w_out,
                state, conv_past):
    r = DEC_BATCH
    outs = [jax.ShapeDtypeStruct((r, D_INNER), F32), jax.ShapeDtypeStruct((r, SSD_CONV_DIM), F32),
            jax.ShapeDtypeStruct((r, LANES), F32), jax.ShapeDtypeStruct((r, SSD_CONV_DIM), F32)]
    z, xbc, dt, u = _call_whole(_ssd_inproj_sample_body, "ssd_inproj_sample", outs,
                                x, nw, w_z, w_x, w_dt, conv_w, conv_b, dt_bias,
                                conv_past[:, 0], conv_past[:, 1], conv_past[:, 2])
    new_conv = jnp.concatenate([conv_past[:, 1:], u[:, None]], axis=1)
    st_shape = (r, SSD_PAIRS, 2 * SSD_HEAD_DIM, SSD_STATE)
    st_spec = pl.BlockSpec((STATE_SEQS,) + st_shape[1:], lambda b: (b, 0, 0, 0))
    vec = lambda w: pl.BlockSpec((STATE_SEQS, 1, w), lambda b: (b, 0, 0))
    const = lambda w: pl.BlockSpec((1, w), lambda b: (0, 0))
    y, new_state = pl.pallas_call(
        _ssd_state_sample_body,
        grid=(r // STATE_SEQS,),
        in_specs=[vec(SSD_CONV_DIM), vec(LANES), const(LANES), const(D_INNER), st_spec],
        out_specs=[vec(D_INNER), st_spec],
        out_shape=[jax.ShapeDtypeStruct((r, 1, D_INNER), F32), jax.ShapeDtypeStruct(st_shape, F32)],
        compiler_params=_params("arbitrary"),
        name="ssd_state_sample",
    )(xbc.reshape(r, 1, SSD_CONV_DIM), dt.reshape(r, 1, LANES), a_log, d_skip,
      state.reshape(st_shape))
    (x_new,) = _call_whole(_ssd_out_sample_body, "ssd_out_sample",
                           [jax.ShapeDtypeStruct((r, D_MODEL), F32)],
                           x, y.reshape(r, D_INNER), z, norm_w, w_out)
    return x_new, new_conv, new_state.reshape(r, SSD_HEADS, SSD_HEAD_DIM, SSD_STATE)


def _qkv_sample_body(x_ref, nw_ref, wt_ref, cos_ref, sdn_ref, sup_ref, qt_ref, kt_ref, vt_ref, sown_ref):
    rows = x_ref.shape[0]
    tab = lambda r: jnp.broadcast_to(r[...], (ATTN_HEAD_DIM, rows))
    qt, kt, vt = _qkv_t(x_ref[...], nw_ref[...], wt_ref, tab(cos_ref), tab(sdn_ref), tab(sup_ref))
    qt_ref[...] = qt
    kt_ref[...] = kt
    vt_ref[...] = vt
    per_head = (qt * kt).reshape(ATTN_HEADS, ATTN_HEAD_DIM, rows)
    sown_ref[...] = jnp.sum(per_head, axis=1) * ATTN_SCALE


N_PAST_BLK = PAST_LEN // MOBA_BLOCK
PAGES_PER_BLK = MOBA_BLOCK // PAGE_SIZE
HEAD3 = (ATTN_HEADS, ATTN_HEAD_DIM, LANES)


def _moba_sample_body(pt_ref, qt_ref, vnt_ref, sown_ref, *refs):
    del pt_ref
    k_refs, v_refs = refs[:N_PAGES], refs[N_PAGES:2 * N_PAGES]
    ot_ref, s_ref, p_ref = refs[2 * N_PAGES:]
    b = pl.program_id(0)
    blk = MOBA_BLOCK
    is_seq = lax.broadcasted_iota(jnp.int32, (D_MODEL, LANES), 1) == b
    lane = lax.broadcasted_iota(jnp.int32, (ATTN_HEADS, LANES), 1)

    @pl.when(b == 0)
    def _():
        ot_ref[...] = jnp.zeros_like(ot_ref)

    def column(src_ref):
        col = jnp.sum(jnp.where(is_seq, src_ref[...], 0.0), axis=1, keepdims=True)
        return jnp.broadcast_to(col, (D_MODEL, LANES)).reshape(HEAD3)

    qb = column(qt_ref)
    gate = jnp.zeros((ATTN_HEADS, LANES), F32)
    for j in range(N_PAST_BLK):
        raw_sum = jnp.zeros((ATTN_HEADS, 1), F32)
        for w in range(PAGES_PER_BLK):
            page = j * PAGES_PER_BLK + w
            s = jnp.sum(k_refs[page][0] * qb, axis=1)
            s_ref[:, page * PAGE_SIZE:(page + 1) * PAGE_SIZE] = s * ATTN_SCALE
            raw_sum = raw_sum + jnp.sum(s, axis=1, keepdims=True)
        gate = jnp.where(lane == j, raw_sum / blk, gate)

    s_own = jnp.sum(jnp.where(lane == b, sown_ref[...], 0.0), axis=1, keepdims=True)
    m = s_own
    masked = []
    for j in range(N_PAST_BLK):
        sel = _moba_select(gate, j, N_PAST_BLK)
        sb = jnp.where(sel, s_ref[:, j * blk:(j + 1) * blk], NEG_INF)
        masked.append(sb)
        m = jnp.maximum(m, jnp.max(sb, axis=1, keepdims=True))
    p_own = jnp.exp(s_own - m)
    l = p_own
    ps = []
    for sb in masked:
        pb = jnp.exp(sb - m)
        ps.append(pb)
        l = l + jnp.sum(pb, axis=1, keepdims=True)
    for j in range(N_PAST_BLK):
        p_ref[:, j * blk:(j + 1) * blk] = ps[j] / l

    first_lane = lax.broadcasted_iota(jnp.int32, HEAD3, 2) == 0
    w_own = jnp.broadcast_to(p_own / l, (ATTN_HEADS, LANES)).reshape(ATTN_HEADS, 1, LANES)
    acc = jnp.where(first_lane, w_own * column(vnt_ref), 0.0)
    for page in range(N_PAGES):
        p = p_ref[:, page * PAGE_SIZE:(page + 1) * PAGE_SIZE]
        acc = acc + p.reshape(ATTN_HEADS, 1, PAGE_SIZE) * v_refs[page][0]
    o_col = jnp.sum(acc, axis=2, keepdims=True)
    o_full = jnp.broadcast_to(o_col, HEAD3).reshape(D_MODEL, LANES)
    ot_ref[...] = jnp.where(is_seq, o_full, ot_ref[...])


def _moba_sample(qt, vnt, s_own, cache_kt, cache_vt, page_table):
    r = DEC_BATCH
    whole2 = pl.BlockSpec((D_MODEL, r), lambda b, pt: (0, 0))
    page = lambda w: pl.BlockSpec((1,) + HEAD3, lambda b, pt: (pt[b * N_PAGES + w], 0, 0, 0))
    pages = [page(w) for w in range(N_PAGES)]
    grid_spec = pltpu.PrefetchScalarGridSpec(
        num_scalar_prefetch=1,
        grid=(r,),
        in_specs=[whole2, whole2, pl.BlockSpec((ATTN_HEADS, r), lambda b, pt: (0, 0))] + pages + pages,
        out_specs=whole2,
        scratch_shapes=[
            pltpu.VMEM((ATTN_HEADS, PAST_LEN), F32),
            pltpu.VMEM((ATTN_HEADS, PAST_LEN), F32),
        ],
    )
    return pl.pallas_call(
        _moba_sample_body,
        grid_spec=grid_spec,
        out_shape=jax.ShapeDtypeStruct((D_MODEL, r), F32),
        compiler_params=_params("arbitrary"),
        name="moba_sample",
    )(page_table.reshape(-1), qt, vnt, s_own, *([cache_kt] * N_PAGES), *([cache_vt] * N_PAGES))


def _rope_tables(pos):
    half = ROT_DIM // 2
    inv_freq = ROPE_THETA ** (-(jnp.arange(half, dtype=F32) * 2.0) / ROT_DIM)
    ang = pos.astype(F32)[:, None] * inv_freq
    cos, sin = jnp.cos(ang), jnp.sin(ang)
    ones = jnp.ones((pos.shape[0], ATTN_HEAD_DIM - ROT_DIM), F32)
    zeros = jnp.zeros((pos.shape[0], ATTN_HEAD_DIM - ROT_DIM), F32)
    zh = jnp.zeros_like(sin)
    return (jnp.concatenate([cos, cos, ones], axis=1),
            jnp.concatenate([-sin, zh, zeros], axis=1),
            jnp.concatenate([zh, sin, zeros], axis=1))


def _pad_lanes(a):
    return jnp.pad(a, [(0, 0)] * (a.ndim - 1) + [(0, LANES - a.shape[-1])])


def kernel(x_prompt, x_sample, state_sconv, state_ssm, state_ssm_conv, cache_k, cache_v, state_ffn_conv, page_table, norm_mix_w, norm_ffn_w, norm_final_w, sconv_w_in, sconv_conv_w, sconv_w_out, ssd_w_in, ssd_conv_w, ssd_conv_b, ssd_dt_bias, ssd_a_log, ssd_d, ssd_norm_w, ssd_w_out, attn_w_qkv, attn_w_o, ffn_w_up, ffn_conv_w, ffn_conv_b, ffn_w_down):
    xp = x_prompt.reshape(N_TOK, D_MODEL)
    xs = x_sample.reshape(DEC_BATCH, D_MODEL)
    fnw = norm_final_w.reshape(1, D_MODEL)
    sconv_p, sconv_s, ffnc_p, ffnc_s = [], [], [], []
    for i in range(DEPTH):
        kind, j = i % N_MIXERS, i // N_MIXERS
        nw = norm_mix_w[i].reshape(1, D_MODEL)
        if kind == 0:
            w_in, w_out = sconv_w_in[j].astype(BF16), sconv_w_out[j].astype(BF16)
            xp, st = _sconv_prompt(xp, nw, w_in, sconv_conv_w[j], w_out)
            xs, st_s = _sconv_sample(xs, nw, w_in, sconv_conv_w[j], w_out, state_sconv[j])
            sconv_p.append(st)
            sconv_s.append(st_s)
        elif kind == 1:
            w_in = ssd_w_in[j]
            w_z = w_in[:, :D_INNER].astype(BF16)
            w_x = w_in[:, D_INNER:D_INNER + SSD_CONV_DIM].astype(BF16)
            w_dt = _pad_lanes(w_in[:, D_INNER + SSD_CONV_DIM:]).astype(BF16)
            w_out = ssd_w_out[j].astype(BF16)
            conv_b = ssd_conv_b[j].reshape(1, SSD_CONV_DIM)
            dt_bias = _pad_lanes(ssd_dt_bias[j].reshape(1, SSD_HEADS))
            a_log = _pad_lanes(ssd_a_log[j].reshape(1, SSD_HEADS))
            d_skip = jnp.repeat(ssd_d[j], SSD_HEAD_DIM).reshape(1, D_INNER)
            norm_w = ssd_norm_w[j].reshape(1, D_INNER)
            z, xbc, dt, ssmc_p = _ssd_inproj_prompt(xp, nw, w_z, w_x, w_dt, ssd_conv_w[j], conv_b, dt_bias)
            d_skip_rows = jnp.broadcast_to(d_skip.reshape(D_INNER, 1), (D_INNER, LANES))
            xp, ssm_p = _ssd_scan_prompt(xp, z, xbc, dt, a_log, d_skip_rows, norm_w, w_out)
            ssm_p = ssm_p.reshape(BATCH, SSD_HEADS, SSD_HEAD_DIM, SSD_STATE)
            xs, ssmc_s, ssm_s = _ssd_sample(xs, nw, w_z, w_x, w_dt, ssd_conv_w[j], conv_b, dt_bias,
                                            a_log, d_skip, norm_w, w_out, state_ssm[j], state_ssm_conv[j])
        else:
            w_qkv, w_o = attn_w_qkv[j].astype(BF16), attn_w_o[j].astype(BF16)
            w_qkv_t = w_qkv.T
            tabs = [t.T for t in _rope_tables(jnp.arange(SEQ))]
            qt, kt_p, vt_p, k_rows = _qkv_prompt(xp, nw, w_qkv_t, *tabs)
            xp = _proj_residual_t(_moba_prompt(qt, k_rows, vt_p), w_o, xp, PROJ_TL)
            tabs = [t.T for t in _rope_tables(jnp.full((1,), PAST_LEN))]
            sds = jax.ShapeDtypeStruct((D_MODEL, DEC_BATCH), F32)
            own = jax.ShapeDtypeStruct((ATTN_HEADS, DEC_BATCH), F32)
            qt_s, kt_s, vt_s, s_own = _call_whole(_qkv_sample_body, "qkv_sample", [sds, sds, sds, own],
                                                  xs, nw, w_qkv_t, *tabs)
            pages = lambda c: jnp.transpose(c, (0, 2, 3, 1))
            ot_s = _moba_sample(qt_s, vt_s, s_own, pages(cache_k[j]), pages(cache_v[j]), page_table)
            xs = _proj_residual_t(ot_s[None], w_o, xs, DEC_BATCH)
        nwf = norm_ffn_w[i].reshape(1, D_MODEL)
        w_up, w_down = ffn_w_up[i].astype(BF16), ffn_w_down[i].astype(BF16)
        conv_b = ffn_conv_b[i].reshape(1, 2 * D_FF)
        final = i == DEPTH - 1
        xp, fc_p = _ffn_prompt(xp, nwf, w_up, ffn_conv_w[i], conv_b, w_down, fnw, final)
        xs, fc_s = _ffn_sample(xs, nwf, w_up, ffn_conv_w[i], conv_b, w_down, fnw, state_ffn_conv[i], final)
        ffnc_p.append(fc_p)
        ffnc_s.append(fc_s)
    heads = (ATTN_HEADS, ATTN_HEAD_DIM)
    kv_p = lambda t: jnp.transpose(t.reshape((BATCH,) + heads + (SEQ,)), (0, 3, 1, 2))[None]
    kv_s = lambda t: jnp.transpose(t.reshape(heads + (DEC_BATCH,)), (2, 0, 1))[None, :, None]
    return (xp.reshape(BATCH, SEQ, D_MODEL), xs.reshape(DEC_BATCH, 1, D_MODEL),
            jnp.stack(sconv_p), jnp.stack(sconv_s),
            ssm_p[None], ssm_s[None], ssmc_p[None], ssmc_s[None],
            kv_p(kt_p), kv_p(vt_p), kv_s(kt_s), kv_s(vt_s),
            jnp.stack(ffnc_p), jnp.stack(ffnc_s))
```
